```python
import jax, jax.numpy as jnp
from jax import lax
import numpy as np

D_MODEL = 1024
BATCH = 8
SEQ = 8192
DEPTH = 1
DEC_BATCH = 16
DEC_SEQ = 32
PAST_LEN = 2048

CHUNK = 64
SGU_CHUNK = 128
N_SGU_HEADS = 4
SGU_WIDTH = D_MODEL // 2
SGU_HEAD_DIM = SGU_WIDTH // N_SGU_HEADS
N_CONV_GROUPS = 4
CONV_WIDTH = D_MODEL // 2
CONV_K = 3
MIX_WIDTH = SGU_WIDTH + CONV_WIDTH
IN_PROJ = 2 * SGU_WIDTH + 3 * CONV_WIDTH
N_MEM = 256
N_MEM_HEADS = 4
MEM_HEAD_DIM = D_MODEL // N_MEM_HEADS
D_FF = -(-8 * D_MODEL // (3 * 256)) * 256
ALPHA = (2.0 * DEPTH) ** 0.25
BETA = (8.0 * DEPTH) ** -0.25
LN_EPS = 1e-5

kernel_name = 'hybrid_sgu_shortconv_memxattn_step'


def _layernorm(x, g, b):
    xf = x.astype(jnp.float32)
    mu = jnp.mean(xf, axis=-1, keepdims=True)
    xc = xf - mu
    var = jnp.mean(jnp.square(xc), axis=-1, keepdims=True)
    return (xc * lax.rsqrt(var + LN_EPS) * g + b).astype(x.dtype)


def _chunk_mask(n):
    idx = jnp.arange(n) // CHUNK
    return idx[None, :] <= idx[:, None]


def _sgu_prompt(u, v, w_s, b_s):
    bsz, s = v.shape[0], v.shape[1]
    n = s // SGU_CHUNK
    w = jnp.where(_chunk_mask(SGU_CHUNK)[None], w_s, 0.0).astype(v.dtype)
    vc = v.reshape(bsz, n, SGU_CHUNK, N_SGU_HEADS, SGU_HEAD_DIM)
    mixed = jnp.einsum('hij,bnjhc->bnihc', w, vc) + b_s.T[None, None, :, :, None]
    return u * mixed.reshape(u.shape)


def _sgu_sample(u, v, w_s, b_s):
    t = v.shape[1]
    w = jnp.where(_chunk_mask(t)[None], w_s[:, :t, :t], 0.0).astype(v.dtype)
    mixed = jnp.einsum('hij,bjhc->bihc', w, v) + b_s[:, :t].T[None, :, :, None]
    return u * mixed


def _layer(x, conv_prev, mem_k, mem_v, sgu_fn,
           w_in, sgu_ln_g, sgu_ln_b, w_s, b_s, w_conv, w_out, ln_mix_g, ln_mix_b,
           w_q, w_mem_o, ln_mem_g, ln_mem_b,
           w_gate, w_up, w_down, ln_ffn_g, ln_ffn_b):
    bsz, s, _ = x.shape
    z = x @ w_in
    u, v, gb, gc, xin = jnp.split(
        z, np.cumsum([SGU_WIDTH, SGU_WIDTH, CONV_WIDTH, CONV_WIDTH]).tolist(), axis=-1)
    u = u.reshape(bsz, s, N_SGU_HEADS, SGU_HEAD_DIM)
    v = _layernorm(v.reshape(bsz, s, N_SGU_HEADS, SGU_HEAD_DIM), sgu_ln_g, sgu_ln_b)
    a_out = sgu_fn(u, v, w_s, b_s).reshape(bsz, s, SGU_WIDTH)
    cx = gc * xin
    cat = jnp.concatenate([conv_prev, cx], axis=1)
    conv = (cat[:, 0:s] * w_conv[0] + cat[:, 1:s + 1] * w_conv[1]
            + cat[:, 2:s + 2] * w_conv[2])
    conv_state = cat[:, -(CONV_K - 1):]
    b_out = gb * conv
    mix = jnp.concatenate([a_out, b_out], axis=-1) @ w_out
    x = _layernorm(ALPHA * x + mix, ln_mix_g, ln_mix_b)
    q = (x @ w_q).reshape(bsz, s, N_MEM_HEADS, MEM_HEAD_DIM)
    sc = jnp.einsum('bshd,bmhd->bhsm', q, mem_k).astype(jnp.float32) * (MEM_HEAD_DIM ** -0.5)
    p = jax.nn.softmax(sc, axis=-1).astype(x.dtype)
    o = jnp.einsum('bhsm,bmhd->bshd', p, mem_v).reshape(bsz, s, D_MODEL)
    x = _layernorm(ALPHA * x + o @ w_mem_o, ln_mem_g, ln_mem_b)
    f = (jax.nn.silu(x @ w_gate) * (x @ w_up)) @ w_down
    x = _layernorm(ALPHA * x + f, ln_ffn_g, ln_ffn_b)
    return x, conv_state, v


def setup_inputs(seed: int = 0) -> dict:
    key = jax.random.key(seed)
    ks = iter(jax.random.split(key, 40))
    nrm = lambda shape, scale: jax.random.normal(next(ks), shape, jnp.float32) * scale
    gain = lambda shape: 1.0 + nrm(shape, 0.01)
    L = DEPTH
    return {
        'x_prompt': nrm((BATCH, SEQ, D_MODEL), 1.0),
        'x_sample': nrm((DEC_BATCH, DEC_SEQ, D_MODEL), 1.0),
        'cache_mem_k': nrm((L, DEC_BATCH, N_MEM, N_MEM_HEADS, MEM_HEAD_DIM), 1.0),
        'cache_mem_v': nrm((L, DEC_BATCH, N_MEM, N_MEM_HEADS, MEM_HEAD_DIM), BETA),
        'state_conv': nrm((L, DEC_BATCH, CONV_K - 1, CONV_WIDTH), 1.0),
        'mem_prompt': nrm((BATCH, N_MEM, D_MODEL), 1.0),
        'w_in': nrm((L, D_MODEL, IN_PROJ), D_MODEL ** -0.5),
        'sgu_ln_g': gain((L, N_SGU_HEADS, SGU_HEAD_DIM)),
        'sgu_ln_b': nrm((L, N_SGU_HEADS, SGU_HEAD_DIM), 0.01),
        'w_s': nrm((L, N_SGU_HEADS, SGU_CHUNK, SGU_CHUNK), SGU_CHUNK ** -0.5),
        'b_s': gain((L, N_SGU_HEADS, SGU_CHUNK)),
        'w_conv': nrm((L, CONV_K, CONV_WIDTH), CONV_K ** -0.5),
        'w_out': nrm((L, MIX_WIDTH, D_MODEL), BETA * MIX_WIDTH ** -0.5),
        'ln_mix_g': gain((L, D_MODEL)),
        'ln_mix_b': nrm((L, D_MODEL), 0.01),
        'w_q': nrm((L, D_MODEL, D_MODEL), D_MODEL ** -0.5),
        'w_mem_k': nrm((L, D_MODEL, D_MODEL), D_MODEL ** -0.5),
        'w_mem_v': nrm((L, D_MODEL, D_MODEL), BETA * D_MODEL ** -0.5),
        'w_mem_o': nrm((L, D_MODEL, D_MODEL), BETA * D_MODEL ** -0.5),
        'ln_mem_g': gain((L, D_MODEL)),
        'ln_mem_b': nrm((L, D_MODEL), 0.01),
        'w_gate': nrm((L, D_MODEL, D_FF), D_MODEL ** -0.5),
        'w_up': nrm((L, D_MODEL, D_FF), D_MODEL ** -0.5),
        'w_down': nrm((L, D_FF, D_MODEL), BETA * D_FF ** -0.5),
        'ln_ffn_g': gain((L, D_MODEL)),
        'ln_ffn_b': nrm((L, D_MODEL), 0.01),
    }


def reference(x_prompt, x_sample, cache_mem_k, cache_mem_v, state_conv, mem_prompt,
              w_in, sgu_ln_g, sgu_ln_b, w_s, b_s, w_conv, w_out, ln_mix_g, ln_mix_b,
              w_q, w_mem_k, w_mem_v, w_mem_o, ln_mem_g, ln_mem_b,
              w_gate, w_up, w_down, ln_ffn_g, ln_ffn_b):
    hp, hs = x_prompt, x_sample
    bp = x_prompt.shape[0]
    mk_p, mv_p, cv_p, cv_s, sv_s = [], [], [], [], []
    for l in range(DEPTH):
        shared = (w_in[l], sgu_ln_g[l], sgu_ln_b[l], w_s[l], b_s[l], w_conv[l], w_out[l],
                  ln_mix_g[l], ln_mix_b[l], w_q[l], w_mem_o[l], ln_mem_g[l], ln_mem_b[l],
                  w_gate[l], w_up[l], w_down[l], ln_ffn_g[l], ln_ffn_b[l])
        mk = (mem_prompt @ w_mem_k[l]).reshape(bp, N_MEM, N_MEM_HEADS, MEM_HEAD_DIM)
        mv = (mem_prompt @ w_mem_v[l]).reshape(bp, N_MEM, N_MEM_HEADS, MEM_HEAD_DIM)
        pad = jnp.zeros((bp, CONV_K - 1, CONV_WIDTH), hp.dtype)
        hp, cp, _ = _layer(hp, pad, mk, mv, _sgu_prompt, *shared)
        hs, cs, vs = _layer(hs, state_conv[l], cache_mem_k[l], cache_mem_v[l], _sgu_sample, *shared)
        mk_p.append(mk); mv_p.append(mv); cv_p.append(cp); cv_s.append(cs); sv_s.append(vs)
    mem_k_prompt = jnp.stack(mk_p)
    mem_v_prompt = jnp.stack(mv_p)
    conv_prompt = jnp.stack(cv_p)
    conv_sample = jnp.stack(cv_s)
    sgu_v_sample = jnp.stack(sv_s)
    return (hp, hs, mem_k_prompt, mem_v_prompt, conv_prompt, conv_sample, sgu_v_sample)
```

```python
import functools

import jax
import jax.numpy as jnp
from jax import lax
from jax.experimental import pallas as pl
from jax.experimental.pallas import tpu as pltpu

D_MODEL = 1024
CHUNK = 64
SGU_CHUNK = 128
N_SGU_HEADS = 4
SGU_WIDTH = D_MODEL // 2
SGU_HEAD_DIM = SGU_WIDTH // N_SGU_HEADS
CONV_WIDTH = D_MODEL // 2
CONV_K = 3
IN_PROJ = 2 * SGU_WIDTH + 3 * CONV_WIDTH
N_MEM = 256
N_MEM_HEADS = 4
MEM_HEAD_DIM = D_MODEL // N_MEM_HEADS
D_FF = 2816
ALPHA = 2.0 ** 0.25
LN_EPS = 1e-5
ATTN_SCALE = MEM_HEAD_DIM ** -0.5

SUBLANES = 8
MXU_WIDTH = 256
TILE_ROWS = 512
VMEM_LIMIT_BYTES = 60 * 1024 * 1024

F32 = jnp.float32
BF16 = jnp.bfloat16

_U0, _V0, _GB0, _GC0, _XI0 = 0, SGU_WIDTH, 2 * SGU_WIDTH, 2 * SGU_WIDTH + CONV_WIDTH, 2 * SGU_WIDTH + 2 * CONV_WIDTH


def _dot(a, b):
    return jnp.dot(a, b, preferred_element_type=F32)


def _layernorm(x, g, b):
    mu = jnp.mean(x, axis=-1, keepdims=True)
    xc = x - mu
    var = jnp.mean(xc * xc, axis=-1, keepdims=True)
    return xc * lax.rsqrt(var + LN_EPS) * g + b


def _shift_rows(cur, prev8, k):
    rolled = pltpu.roll(cur, k, axis=0)
    head = pltpu.roll(prev8, k, axis=0)
    row = lax.broadcasted_iota(jnp.int32, (SUBLANES, cur.shape[1]), 0)
    first = jnp.where(row < k, head, rolled[:SUBLANES])
    return jnp.concatenate([first, rolled[SUBLANES:]], axis=0)


def _sgu_norm(z_ref, sgu_g_ref, sgu_b_ref, rows):
    parts = []
    for h in range(N_SGU_HEADS):
        c0 = _V0 + h * SGU_HEAD_DIM
        parts.append(_layernorm(z_ref[rows, c0:c0 + SGU_HEAD_DIM],
                                sgu_g_ref[h:h + 1, :], sgu_b_ref[h:h + 1, :]))
    return parts


def _attend(q_ref, k_ref, v_ref, o_ref, rows):
    for h in range(N_MEM_HEADS):
        c = slice(h * MEM_HEAD_DIM, (h + 1) * MEM_HEAD_DIM)
        sc = lax.dot_general(q_ref[rows, c], k_ref[:, c].astype(BF16),
                             (((1,), (1,)), ((), ())), preferred_element_type=F32) * ATTN_SCALE
        e = jnp.exp(sc - jnp.max(sc, axis=-1, keepdims=True))
        p = e / jnp.sum(e, axis=-1, keepdims=True)
        o_ref[rows, c] = _dot(p.astype(BF16), v_ref[:, c].astype(BF16)).astype(BF16)


def _ffn(x2_ref, xb_ref, w_gate_ref, w_up_ref, w_down_ref, g_ref, b_ref, h_ref):
    for j in range(D_FF // MXU_WIDTH):
        c = slice(j * MXU_WIDTH, (j + 1) * MXU_WIDTH)
        gate = _dot(xb_ref[...], w_gate_ref[:, c])
        up = _dot(xb_ref[...], w_up_ref[:, c])
        h_ref[:, c] = (gate * jax.nn.sigmoid(gate) * up).astype(BF16)
    f = _dot(h_ref[...], w_down_ref[...])
    return _layernorm(ALPHA * x2_ref[...] + f, g_ref[...], b_ref[...])


def _masked_mix_weights(w_s_ref, t):
    row = lax.broadcasted_iota(jnp.int32, (SGU_CHUNK, SGU_CHUNK), 0)
    col = lax.broadcasted_iota(jnp.int32, (SGU_CHUNK, SGU_CHUNK), 1)
    chunk_of = lambda i: lax.shift_right_logical(i, CHUNK.bit_length() - 1)
    out = []
    for h in range(N_SGU_HEADS):
        w = w_s_ref[h]
        if t == SGU_CHUNK:
            keep = chunk_of(col) <= chunk_of(row)
        else:
            w = jnp.where(col < t, jnp.concatenate([w[:t]] * (SGU_CHUNK // t), axis=0), 0.0)
            acc = w
            for r in range(1, SGU_CHUNK // t):
                acc = acc + pltpu.roll(w, r * t, axis=1)
            w = acc
            block_of = lambda i: lax.shift_right_logical(i, t.bit_length() - 1)
            within = lambda i: i & (t - 1)
            keep = (block_of(col) == block_of(row)) & (chunk_of(within(col)) <= chunk_of(within(row)))
        out.append(jnp.where(keep, w, 0.0).astype(BF16))
    return out


def _mixers(z_ref, wm, bias_ref, w_conv_ref, vn_fn, prev8_fn, mix_ref, n_rows, stream_rows):
    cx_prev = None
    for c in range(n_rows // SGU_CHUNK):
        rows = slice(c * SGU_CHUNK, (c + 1) * SGU_CHUNK)
        vn = vn_fn(rows)
        for h in range(N_SGU_HEADS):
            cols = slice(h * SGU_HEAD_DIM, (h + 1) * SGU_HEAD_DIM)
            mixed = _dot(wm[h], vn[h].astype(BF16)) + bias_ref[:, cols]
            u = z_ref[rows, _U0 + h * SGU_HEAD_DIM:_U0 + (h + 1) * SGU_HEAD_DIM]
            mix_ref[rows, cols] = (u * mixed).astype(BF16)
        cx = z_ref[rows, _GC0:_GC0 + CONV_WIDTH] * z_ref[rows, _XI0:_XI0 + CONV_WIDTH]
        if stream_rows >= SGU_CHUNK:
            prev8 = prev8_fn(c) if (c * SGU_CHUNK) % stream_rows == 0 else cx_prev[-SUBLANES:]
            s1 = _shift_rows(cx, prev8, 1)
            s2 = _shift_rows(cx, prev8, 2)
        else:
            s1p, s2p = [], []
            for s in range(SGU_CHUNK // stream_rows):
                blk = cx[s * stream_rows:(s + 1) * stream_rows]
                prev8 = prev8_fn(c * (SGU_CHUNK // stream_rows) + s)
                s1p.append(_shift_rows(blk, prev8, 1))
                s2p.append(_shift_rows(blk, prev8, 2))
            s1 = jnp.concatenate(s1p, axis=0)
            s2 = jnp.concatenate(s2p, axis=0)
        conv = s2 * w_conv_ref[0:1, :] + s1 * w_conv_ref[1:2, :] + cx * w_conv_ref[2:3, :]
        mix_ref[rows, SGU_WIDTH:] = (z_ref[rows, _GB0:_GB0 + CONV_WIDTH] * conv).astype(BF16)
        cx_prev = cx
    return cx_prev


def _prompt_kernel(x_ref, mk_ref, mv_ref, w_in_ref, sgu_g_ref, sgu_b_ref, w_s_ref, bias_ref,
                   w_conv_ref, w_out_ref, ln1g_ref, ln1b_ref, w_q_ref, w_o_ref, ln2g_ref, ln2b_ref,
                   w_gate_ref, w_up_ref, w_down_ref, ln3g_ref, ln3b_ref,
                   y_ref, conv_ref,
                   z_ref, act_a, act_b, h_ref, res_ref, carry_ref):
    @pl.when(pl.program_id(1) == 0)
    def _():
        carry_ref[...] = jnp.zeros_like(carry_ref)

    z_ref[...] = _dot(x_ref[...].astype(BF16), w_in_ref[...])
    wm = _masked_mix_weights(w_s_ref, SGU_CHUNK)
    cx_last = _mixers(z_ref, wm, bias_ref, w_conv_ref,
                      lambda rows: _sgu_norm(z_ref, sgu_g_ref, sgu_b_ref, rows),
                      lambda c: carry_ref[...], act_a, TILE_ROWS, TILE_ROWS)
    carry_ref[...] = cx_last[-SUBLANES:]
    conv_ref[...] = cx_last[-SUBLANES:]

    x1 = _layernorm(ALPHA * x_ref[...] + _dot(act_a[...], w_out_ref[...]), ln1g_ref[...], ln1b_ref[...])
    res_ref[...] = x1
    act_b[...] = x1.astype(BF16)
    act_a[...] = _dot(act_b[...], w_q_ref[...]).astype(BF16)
    _attend(act_a, mk_ref, mv_ref, act_b, slice(None))
    x2 = _layernorm(ALPHA * res_ref[...] + _dot(act_b[...], w_o_ref[...]), ln2g_ref[...], ln2b_ref[...])
    res_ref[...] = x2
    act_a[...] = x2.astype(BF16)
    y_ref[...] = _ffn(res_ref, act_a, w_gate_ref, w_up_ref, w_down_ref, ln3g_ref, ln3b_ref, h_ref)


def _const_spec(shape):
    return pl.BlockSpec(shape, lambda *_: (0,) * len(shape), pipeline_mode=pl.Buffered(1))


def _whole_spec(shape):
    return pl.BlockSpec(shape, lambda *_: (0,) * len(shape))


def _prompt_layer(x, mk, mv, p):
    bsz, seq, _ = x.shape
    n_tiles = seq // TILE_ROWS
    weights = (p['w_in'], p['sgu_g'], p['sgu_b'], p['w_s'], p['bias'], p['w_conv'], p['w_out'],
               p['ln1g'], p['ln1b'], p['w_q'], p['w_o'], p['ln2g'], p['ln2b'],
               p['w_gate'], p['w_up'], p['w_down'], p['ln3g'], p['ln3b'])
    kv_spec = pl.BlockSpec((None, N_MEM, D_MODEL), lambda b, s: (b, 0, 0))
    return pl.pallas_call(
        _prompt_kernel,
        grid=(bsz, n_tiles),
        in_specs=[pl.BlockSpec((None, TILE_ROWS, D_MODEL), lambda b, s: (b, s, 0)), kv_spec, kv_spec]
        + [_const_spec(w.shape) for w in weights],
        out_specs=[pl.BlockSpec((None, TILE_ROWS, D_MODEL), lambda b, s: (b, s, 0)),
                   pl.BlockSpec((None, SUBLANES, CONV_WIDTH), lambda b, s: (b, 0, 0))],
        out_shape=[jax.ShapeDtypeStruct((bsz, seq, D_MODEL), F32),
                   jax.ShapeDtypeStruct((bsz, SUBLANES, CONV_WIDTH), F32)],
        scratch_shapes=[pltpu.VMEM((TILE_ROWS, IN_PROJ), F32),
                        pltpu.VMEM((TILE_ROWS, D_MODEL), BF16),
                        pltpu.VMEM((TILE_ROWS, D_MODEL), BF16),
                        pltpu.VMEM((TILE_ROWS, D_FF), BF16),
                        pltpu.VMEM((TILE_ROWS, D_MODEL), F32),
                        pltpu.VMEM((SUBLANES, CONV_WIDTH), F32)],
        compiler_params=pltpu.CompilerParams(dimension_semantics=("arbitrary", "arbitrary"),
                                             vmem_limit_bytes=VMEM_LIMIT_BYTES),
        name="prompt_layer",
    )(x, mk, mv, *weights)


def _mem_kv_kernel(m_ref, wk_ref, wv_ref, k_ref, v_ref):
    m = m_ref[...].astype(BF16)
    k_ref[...] = _dot(m, wk_ref[...])
    v_ref[...] = _dot(m, wv_ref[...])


def _mem_kv(mem, wk, wv):
    bsz = mem.shape[0]
    spec = pl.BlockSpec((None, N_MEM, D_MODEL), lambda b: (b, 0, 0))
    wspec = pl.BlockSpec((D_MODEL, D_MODEL), lambda b: (0, 0))
    return pl.pallas_call(
        _mem_kv_kernel, grid=(bsz,), in_specs=[spec, wspec, wspec], out_specs=[spec, spec],
        out_shape=[jax.ShapeDtypeStruct((bsz, N_MEM, D_MODEL), F32)] * 2,
        compiler_params=pltpu.CompilerParams(dimension_semantics=("arbitrary",)),
        name="mem_kv",
    )(mem, wk, wv)


def _sample_a_kernel(x_ref, st_ref, w_in_ref, sgu_g_ref, sgu_b_ref, w_s_ref, bias_ref, w_conv_ref,
                     w_out_ref, ln1g_ref, ln1b_ref, w_q_ref,
                     x1_ref, q_ref, vn_ref, conv_ref, z_ref, mix_ref, *, n_rows, stream_rows):
    x = x_ref[...]
    z_ref[...] = _dot(x.astype(BF16), w_in_ref[...])
    wm = _masked_mix_weights(w_s_ref, stream_rows)

    def vn_fn(rows):
        parts = _sgu_norm(z_ref, sgu_g_ref, sgu_b_ref, rows)
        for h in range(N_SGU_HEADS):
            vn_ref[rows, h * SGU_HEAD_DIM:(h + 1) * SGU_HEAD_DIM] = parts[h]
        return parts

    def prev8_fn(s):
        return st_ref[s]

    _mixers(z_ref, wm, bias_ref, w_conv_ref, vn_fn, prev8_fn, mix_ref, n_rows, stream_rows)
    for s in range(n_rows // stream_rows):
        r0 = (s + 1) * stream_rows - SUBLANES
        conv_ref[s] = (z_ref[r0:r0 + SUBLANES, _GC0:_GC0 + CONV_WIDTH]
                       * z_ref[r0:r0 + SUBLANES, _XI0:_XI0 + CONV_WIDTH])
    x1 = _layernorm(ALPHA * x + _dot(mix_ref[...], w_out_ref[...]), ln1g_ref[...], ln1b_ref[...])
    x1_ref[...] = x1
    q_ref[...] = _dot(x1.astype(BF16), w_q_ref[...]).astype(BF16)


def _sample_attn_kernel(q_ref, k_ref, v_ref, o_ref):
    _attend(q_ref, k_ref, v_ref, o_ref, slice(None))


def _sample_c_kernel(x1_ref, o_ref, w_o_ref, ln2g_ref, ln2b_ref, w_gate_ref, w_up_ref, w_down_ref,
                     ln3g_ref, ln3b_ref, y_ref, xb_ref, h_ref, res_ref):
    x2 = _layernorm(ALPHA * x1_ref[...] + _dot(o_ref[...], w_o_ref[...]), ln2g_ref[...], ln2b_ref[...])
    res_ref[...] = x2
    xb_ref[...] = x2.astype(BF16)
    y_ref[...] = _ffn(res_ref, xb_ref, w_gate_ref, w_up_ref, w_down_ref, ln3g_ref, ln3b_ref, h_ref)


def _sample_layer(x, state8, mem_k, mem_v, p, bias):
    n_streams, stream_rows, _ = x.shape
    n_rows = n_streams * stream_rows
    x2d = x.reshape(n_rows, D_MODEL)
    a_in = (x2d, state8, p['w_in'], p['sgu_g'], p['sgu_b'], p['w_s'], bias, p['w_conv'], p['w_out'],
            p['ln1g'], p['ln1b'], p['w_q'])
    x1, q, vn, conv8 = pl.pallas_call(
        functools.partial(_sample_a_kernel, n_rows=n_rows, stream_rows=stream_rows),
        grid=(1,),
        in_specs=[_const_spec(a.shape) for a in a_in],
        out_specs=[_whole_spec((n_rows, D_MODEL)), _whole_spec((n_rows, D_MODEL)),
                   _whole_spec((n_rows, SGU_WIDTH)), _whole_spec((n_streams, SUBLANES, CONV_WIDTH))],
        out_shape=[jax.ShapeDtypeStruct((n_rows, D_MODEL), F32),
                   jax.ShapeDtypeStruct((n_rows, D_MODEL), BF16),
                   jax.ShapeDtypeStruct((n_rows, SGU_WIDTH), F32),
                   jax.ShapeDtypeStruct((n_streams, SUBLANES, CONV_WIDTH), F32)],
        scratch_shapes=[pltpu.VMEM((n_rows, IN_PROJ), F32), pltpu.VMEM((n_rows, D_MODEL), BF16)],
        compiler_params=pltpu.CompilerParams(dimension_semantics=("arbitrary",),
                                             vmem_limit_bytes=VMEM_LIMIT_BYTES),
        name="sample_mixers",
    )(*a_in)

    qspec = pl.BlockSpec((stream_rows, D_MODEL), lambda s: (s, 0))
    kvspec = pl.BlockSpec((None, N_MEM, D_MODEL), lambda s: (s, 0, 0))
    o = pl.pallas_call(
        _sample_attn_kernel, grid=(n_streams,), in_specs=[qspec, kvspec, kvspec], out_specs=qspec,
        out_shape=jax.ShapeDtypeStruct((n_rows, D_MODEL), BF16),
        compiler_params=pltpu.CompilerParams(dimension_semantics=("arbitrary",)),
        name="sample_attention",
    )(q, mem_k, mem_v)

    c_in = (x1, o, p['w_o'], p['ln2g'], p['ln2b'], p['w_gate'], p['w_up'], p['w_down'],
            p['ln3g'], p['ln3b'])
    y = pl.pallas_call(
        _sample_c_kernel, grid=(1,),
        in_specs=[_const_spec(a.shape) for a in c_in],
        out_specs=_whole_spec((n_rows, D_MODEL)),
        out_shape=jax.ShapeDtypeStruct((n_rows, D_MODEL), F32),
        scratch_shapes=[pltpu.VMEM((n_rows, D_MODEL), BF16), pltpu.VMEM((n_rows, D_FF), BF16),
                        pltpu.VMEM((n_rows, D_MODEL), F32)],
        compiler_params=pltpu.CompilerParams(dimension_semantics=("arbitrary",),
                                             vmem_limit_bytes=VMEM_LIMIT_BYTES),
        name="sample_ffn",
    )(*c_in)
    return y.reshape(x.shape), conv8, vn


def _row_bias(b_s, t):
    b = jnp.tile(b_s[:, :t], (1, SGU_CHUNK // t))
    return jnp.repeat(b.T, SGU_HEAD_DIM, axis=1)


def kernel(x_prompt, x_sample, cache_mem_k, cache_mem_v, state_conv, mem_prompt, w_in, sgu_ln_g, sgu_ln_b, w_s, b_s, w_conv, w_out, ln_mix_g, ln_mix_b, w_q, w_mem_k, w_mem_v, w_mem_o, ln_mem_g, ln_mem_b, w_gate, w_up, w_down, ln_ffn_g, ln_ffn_b):
    depth = w_in.shape[0]
    assert depth == 1
    bsz, seq, _ = x_prompt.shape
    n_streams, dec_seq, _ = x_sample.shape
    assert seq % TILE_ROWS == 0 and SGU_CHUNK % dec_seq == 0 and dec_seq % SUBLANES == 0
    assert (n_streams * dec_seq) % SGU_CHUNK == 0
    l = 0
    row = lambda a: a.reshape(1, -1)
    p = dict(w_in=w_in[l].astype(BF16), sgu_g=sgu_ln_g[l], sgu_b=sgu_ln_b[l], w_s=w_s[l],
             w_conv=w_conv[l], w_out=w_out[l].astype(BF16), ln1g=row(ln_mix_g[l]), ln1b=row(ln_mix_b[l]),
             w_q=w_q[l].astype(BF16), w_o=w_mem_o[l].astype(BF16), ln2g=row(ln_mem_g[l]),
             ln2b=row(ln_mem_b[l]), w_gate=w_gate[l].astype(BF16), w_up=w_up[l].astype(BF16),
             w_down=w_down[l].astype(BF16), ln3g=row(ln_ffn_g[l]), ln3b=row(ln_ffn_b[l]))

    mk, mv = _mem_kv(mem_prompt, w_mem_k[l].astype(BF16), w_mem_v[l].astype(BF16))
    y_prompt, conv8_p = _prompt_layer(x_prompt, mk, mv, dict(p, bias=_row_bias(b_s[l], SGU_CHUNK)))

    state8 = jnp.pad(state_conv[l], ((0, 0), (SUBLANES - (CONV_K - 1), 0), (0, 0)))
    y_sample, conv8_s, vn = _sample_layer(x_sample, state8,
                                          cache_mem_k[l].reshape(n_streams, N_MEM, D_MODEL),
                                          cache_mem_v[l].reshape(n_streams, N_MEM, D_MODEL),
                                          p, _row_bias(b_s[l], dec_seq))

    tail = slice(SUBLANES - (CONV_K - 1), SUBLANES)
    return (y_prompt, y_sample,
            mk.reshape(depth, bsz, N_MEM, N_MEM_HEADS, MEM_HEAD_DIM),
            mv.reshape(depth, bsz, N_MEM, N_MEM_HEADS, MEM_HEAD_DIM),
            conv8_p[None, :, tail], conv8_s[None, :, tail],
            vn.reshape(depth, n_streams, dec_seq, N_SGU_HEADS, SGU_HEAD_DIM))
```

```python
import functools

import jax
import jax.numpy as jnp
from jax import lax
from jax.experimental import pallas as pl
from jax.experimental.pallas import tpu as pltpu

D_MODEL = 1024
CHUNK = 64
SGU_CHUNK = 128
N_SGU_HEADS = 4
SGU_WIDTH = D_MODEL // 2
SGU_HEAD_DIM = SGU_WIDTH // N_SGU_HEADS
CONV_WIDTH = D_MODEL // 2
CONV_K = 3
IN_PROJ = 2 * SGU_WIDTH + 3 * CONV_WIDTH
N_MEM = 256
N_MEM_HEADS = 4
MEM_HEAD_DIM = D_MODEL // N_MEM_HEADS
D_FF = 2816
ALPHA = 2.0 ** 0.25
LN_EPS = 1e-5
ATTN_SCALE = MEM_HEAD_DIM ** -0.5

SUBLANES = 8
MXU_WIDTH = 256
TILE_ROWS = 512
BLOCK_ROWS = 256
VMEM_LIMIT_BYTES = 60 * 1024 * 1024

F32 = jnp.float32
BF16 = jnp.bfloat16

_U0, _V0, _GB0, _GC0, _XI0 = 0, SGU_WIDTH, 2 * SGU_WIDTH, 2 * SGU_WIDTH + CONV_WIDTH, 2 * SGU_WIDTH + 2 * CONV_WIDTH


def _dot(a, b):
    return jnp.dot(a, b, preferred_element_type=F32)


def _layernorm(x, g, b):
    mu = jnp.mean(x, axis=-1, keepdims=True)
    xc = x - mu
    var = jnp.mean(xc * xc, axis=-1, keepdims=True)
    return xc * lax.rsqrt(var + LN_EPS) * g + b


def _shift_rows(cur, prev8, k):
    rolled = pltpu.roll(cur, k, axis=0)
    head = pltpu.roll(prev8, k, axis=0)
    row = lax.broadcasted_iota(jnp.int32, (SUBLANES, cur.shape[1]), 0)
    first = jnp.where(row < k, head, rolled[:SUBLANES])
    return jnp.concatenate([first, rolled[SUBLANES:]], axis=0)


def _sgu_norm(z_ref, sgu_g_ref, sgu_b_ref, rows):
    parts = []
    for h in range(N_SGU_HEADS):
        c0 = _V0 + h * SGU_HEAD_DIM
        parts.append(_layernorm(z_ref[rows, c0:c0 + SGU_HEAD_DIM],
                                sgu_g_ref[h:h + 1, :], sgu_b_ref[h:h + 1, :]))
    return parts


def _attn_scores(q_ref, k_ref, rows):
    scores = []
    for h in range(N_MEM_HEADS):
        c = slice(h * MEM_HEAD_DIM, (h + 1) * MEM_HEAD_DIM)
        scores.append(lax.dot_general(q_ref[rows, c], k_ref[:, c].astype(BF16),
                                      (((1,), (1,)), ((), ())), preferred_element_type=F32) * ATTN_SCALE)
    return scores


def _attn_values(scores, v_ref, o_ref, rows):
    for h, sc in enumerate(scores):
        c = slice(h * MEM_HEAD_DIM, (h + 1) * MEM_HEAD_DIM)
        e = jnp.exp(sc - jnp.max(sc, axis=-1, keepdims=True))
        p = e / jnp.sum(e, axis=-1, keepdims=True)
        o_ref[rows, c] = _dot(p.astype(BF16), v_ref[:, c].astype(BF16)).astype(BF16)


def _ffn(x2_ref, xb_ref, w_gate_ref, w_up_ref, w_down_ref, g_ref, b_ref, h_ref, rows):
    for j in range(D_FF // MXU_WIDTH):
        c = slice(j * MXU_WIDTH, (j + 1) * MXU_WIDTH)
        gate = _dot(xb_ref[rows, :], w_gate_ref[:, c])
        up = _dot(xb_ref[rows, :], w_up_ref[:, c])
        h_ref[rows, c] = (gate * jax.nn.sigmoid(gate) * up).astype(BF16)
    f = _dot(h_ref[rows, :], w_down_ref[...])
    return _layernorm(ALPHA * x2_ref[rows, :] + f, g_ref[...], b_ref[...])


def _masked_mix_weights(w_s_ref, t):
    row = lax.broadcasted_iota(jnp.int32, (SGU_CHUNK, SGU_CHUNK), 0)
    col = lax.broadcasted_iota(jnp.int32, (SGU_CHUNK, SGU_CHUNK), 1)
    chunk_of = lambda i: lax.shift_right_logical(i, CHUNK.bit_length() - 1)
    out = []
    for h in range(N_SGU_HEADS):
        w = w_s_ref[h]
        if t == SGU_CHUNK:
            keep = chunk_of(col) <= chunk_of(row)
        else:
            w = jnp.where(col < t, jnp.concatenate([w[:t]] * (SGU_CHUNK // t), axis=0), 0.0)
            acc = w
            for r in range(1, SGU_CHUNK // t):
                acc = acc + pltpu.roll(w, r * t, axis=1)
            w = acc
            block_of = lambda i: lax.shift_right_logical(i, t.bit_length() - 1)
            within = lambda i: i & (t - 1)
            keep = (block_of(col) == block_of(row)) & (chunk_of(within(col)) <= chunk_of(within(row)))
        out.append(jnp.where(keep, w, 0.0).astype(BF16))
    return out


def _mixers(z_ref, wm, bias_ref, w_conv_ref, vn_fn, prev8_fn, mix_ref, row0, n_rows, stream_rows):
    cx_prev = None
    for c in range(n_rows // SGU_CHUNK):
        rows = slice(row0 + c * SGU_CHUNK, row0 + (c + 1) * SGU_CHUNK)
        vn = vn_fn(rows)
        for h in range(N_SGU_HEADS):
            cols = slice(h * SGU_HEAD_DIM, (h + 1) * SGU_HEAD_DIM)
            mixed = _dot(wm[h], vn[h].astype(BF16)) + bias_ref[:, cols]
            u = z_ref[rows, _U0 + h * SGU_HEAD_DIM:_U0 + (h + 1) * SGU_HEAD_DIM]
            mix_ref[rows, cols] = (u * mixed).astype(BF16)
        cx = z_ref[rows, _GC0:_GC0 + CONV_WIDTH] * z_ref[rows, _XI0:_XI0 + CONV_WIDTH]
        if stream_rows >= SGU_CHUNK:
            prev8 = (prev8_fn((c * SGU_CHUNK) // stream_rows) if (c * SGU_CHUNK) % stream_rows == 0
                     else cx_prev[-SUBLANES:])
            s1 = _shift_rows(cx, prev8, 1)
            s2 = _shift_rows(cx, prev8, 2)
        else:
            s1p, s2p = [], []
            for s in range(SGU_CHUNK // stream_rows):
                blk = cx[s * stream_rows:(s + 1) * stream_rows]
                prev8 = prev8_fn(c * (SGU_CHUNK // stream_rows) + s)
                s1p.append(_shift_rows(blk, prev8, 1))
                s2p.append(_shift_rows(blk, prev8, 2))
            s1 = jnp.concatenate(s1p, axis=0)
            s2 = jnp.concatenate(s2p, axis=0)
        conv = s2 * w_conv_ref[0:1, :] + s1 * w_conv_ref[1:2, :] + cx * w_conv_ref[2:3, :]
        mix_ref[rows, SGU_WIDTH:] = (z_ref[rows, _GB0:_GB0 + CONV_WIDTH] * conv).astype(BF16)
        cx_prev = cx
    return cx_prev[-SUBLANES:]


def _prompt_kernel(x_ref, mk_ref, mv_ref, w_in_ref, sgu_g_ref, sgu_b_ref, w_s_ref, bias_ref,
                   w_conv_ref, w_out_ref, ln1g_ref, ln1b_ref, w_q_ref, w_o_ref, ln2g_ref, ln2b_ref,
                   w_gate_ref, w_up_ref, w_down_ref, ln3g_ref, ln3b_ref,
                   y_ref, conv_ref,
                   z_ref, act_a, act_b, h_ref, res_ref, carry_ref):
    @pl.when(pl.program_id(1) == 0)
    def _():
        carry_ref[...] = jnp.zeros_like(carry_ref)

    wm = _masked_mix_weights(w_s_ref, SGU_CHUNK)
    blocks = [slice(i * BLOCK_ROWS, (i + 1) * BLOCK_ROWS) for i in range(TILE_ROWS // BLOCK_ROWS)]
    for rows in blocks:
        z_ref[rows, :] = _dot(x_ref[rows, :].astype(BF16), w_in_ref[...])
    prev8 = carry_ref[...]
    for rows in blocks:
        prev8 = _mixers(z_ref, wm, bias_ref, w_conv_ref,
                        lambda r: _sgu_norm(z_ref, sgu_g_ref, sgu_b_ref, r),
                        lambda _, prev8=prev8: prev8, act_a, rows.start, BLOCK_ROWS, BLOCK_ROWS)
    carry_ref[...] = prev8
    conv_ref[...] = prev8
    for rows in blocks:
        x1 = _layernorm(ALPHA * x_ref[rows, :] + _dot(act_a[rows, :], w_out_ref[...]),
                        ln1g_ref[...], ln1b_ref[...])
        res_ref[rows, :] = x1
        act_b[rows, :] = x1.astype(BF16)
    for rows in blocks:
        act_a[rows, :] = _dot(act_b[rows, :], w_q_ref[...]).astype(BF16)
    scores = [_attn_scores(act_a, mk_ref, rows) for rows in blocks]
    for rows, sc in zip(blocks, scores):
        _attn_values(sc, mv_ref, act_b, rows)
    for rows in blocks:
        x2 = _layernorm(ALPHA * res_ref[rows, :] + _dot(act_b[rows, :], w_o_ref[...]),
                        ln2g_ref[...], ln2b_ref[...])
        res_ref[rows, :] = x2
        act_a[rows, :] = x2.astype(BF16)
    for rows in blocks:
        y_ref[rows, :] = _ffn(res_ref, act_a, w_gate_ref, w_up_ref, w_down_ref, ln3g_ref, ln3b_ref,
                              h_ref, rows)


def _const_spec(shape):
    return pl.BlockSpec(shape, lambda *_: (0,) * len(shape), pipeline_mode=pl.Buffered(1))


def _whole_spec(shape):
    return pl.BlockSpec(shape, lambda *_: (0,) * len(shape))


def _prompt_layer(x, mk, mv, p):
    bsz, seq, _ = x.shape
    n_tiles = seq // TILE_ROWS
    weights = (p['w_in'], p['sgu_g'], p['sgu_b'], p['w_s'], p['bias'], p['w_conv'], p['w_out'],
               p['ln1g'], p['ln1b'], p['w_q'], p['w_o'], p['ln2g'], p['ln2b'],
               p['w_gate'], p['w_up'], p['w_down'], p['ln3g'], p['ln3b'])
    kv_spec = pl.BlockSpec((None, N_MEM, D_MODEL), lambda b, s: (b, 0, 0))
    return pl.pallas_call(
        _prompt_kernel,
        grid=(bsz, n_tiles),
        in_specs=[pl.BlockSpec((None, TILE_ROWS, D_MODEL), lambda b, s: (b, s, 0)), kv_spec, kv_spec]
        + [_const_spec(w.shape) for w in weights],
        out_specs=[pl.BlockSpec((None, TILE_ROWS, D_MODEL), lambda b, s: (b, s, 0)),
                   pl.BlockSpec((None, SUBLANES, CONV_WIDTH), lambda b, s: (b, 0, 0))],
        out_shape=[jax.ShapeDtypeStruct((bsz, seq, D_MODEL), F32),
                   jax.ShapeDtypeStruct((bsz, SUBLANES, CONV_WIDTH), F32)],
        scratch_shapes=[pltpu.VMEM((TILE_ROWS, IN_PROJ), F32),
                        pltpu.VMEM((TILE_ROWS, D_MODEL), BF16),
                        pltpu.VMEM((TILE_ROWS, D_MODEL), BF16),
                        pltpu.VMEM((TILE_ROWS, D_FF), BF16),
                        pltpu.VMEM((TILE_ROWS, D_MODEL), F32),
                        pltpu.VMEM((SUBLANES, CONV_WIDTH), F32)],
        compiler_params=pltpu.CompilerParams(dimension_semantics=("arbitrary", "arbitrary"),
                                             vmem_limit_bytes=VMEM_LIMIT_BYTES),
        name="prompt_layer",
    )(x, mk, mv, *weights)


def _mem_kv_kernel(m_ref, wk_ref, wv_ref, k_ref, v_ref):
    m = m_ref[...].astype(BF16)
    k_ref[...] = _dot(m, wk_ref[...])
    v_ref[...] = _dot(m, wv_ref[...])


def _mem_kv(mem, wk, wv):
    bsz = mem.shape[0]
    spec = pl.BlockSpec((None, N_MEM, D_MODEL), lambda b: (b, 0, 0))
    wspec = pl.BlockSpec((D_MODEL, D_MODEL), lambda b: (0, 0))
    return pl.pallas_call(
        _mem_kv_kernel, grid=(bsz,), in_specs=[spec, wspec, wspec], out_specs=[spec, spec],
        out_shape=[jax.ShapeDtypeStruct((bsz, N_MEM, D_MODEL), F32)] * 2,
        compiler_params=pltpu.CompilerParams(dimension_semantics=("arbitrary",)),
        name="mem_kv",
    )(mem, wk, wv)


def _sample_a_kernel(x_ref, st_ref, w_in_ref, sgu_g_ref, sgu_b_ref, w_s_ref, bias_ref, w_conv_ref,
                     w_out_ref, ln1g_ref, ln1b_ref, w_q_ref,
                     x1_ref, q_ref, vn_ref, conv_ref, z_ref, mix_ref, *, n_rows, stream_rows):
    x = x_ref[...]
    z_ref[...] = _dot(x.astype(BF16), w_in_ref[...])
    wm = _masked_mix_weights(w_s_ref, stream_rows)

    def vn_fn(rows):
        parts = _sgu_norm(z_ref, sgu_g_ref, sgu_b_ref, rows)
        for h in range(N_SGU_HEADS):
            vn_ref[rows, h * SGU_HEAD_DIM:(h + 1) * SGU_HEAD_DIM] = parts[h]
        return parts

    def prev8_fn(s):
        return st_ref[s]

    _mixers(z_ref, wm, bias_ref, w_conv_ref, vn_fn, prev8_fn, mix_ref, 0, n_rows, stream_rows)
    for s in range(n_rows // stream_rows):
        r0 = (s + 1) * stream_rows - SUBLANES
        conv_ref[s] = (z_ref[r0:r0 + SUBLANES, _GC0:_GC0 + CONV_WIDTH]
                       * z_ref[r0:r0 + SUBLANES, _XI0:_XI0 + CONV_WIDTH])
    x1 = _layernorm(ALPHA * x + _dot(mix_ref[...], w_out_ref[...]), ln1g_ref[...], ln1b_ref[...])
    x1_ref[...] = x1
    q_ref[...] = _dot(x1.astype(BF16), w_q_ref[...]).astype(BF16)


def _sample_attn_kernel(q_ref, k_ref, v_ref, o_ref):
    _attn_values(_attn_scores(q_ref, k_ref, slice(None)), v_ref, o_ref, slice(None))


def _sample_c_kernel(x1_ref, o_ref, w_o_ref, ln2g_ref, ln2b_ref, w_gate_ref, w_up_ref, w_down_ref,
                     ln3g_ref, ln3b_ref, y_ref, xb_ref, h_ref, res_ref):
    x2 = _layernorm(ALPHA * x1_ref[...] + _dot(o_ref[...], w_o_ref[...]), ln2g_ref[...], ln2b_ref[...])
    res_ref[...] = x2
    xb_ref[...] = x2.astype(BF16)
    y_ref[...] = _ffn(res_ref, xb_ref, w_gate_ref, w_up_ref, w_down_ref, ln3g_ref, ln3b_ref, h_ref,
                      slice(None))


def _sample_layer(x, state8, mem_k, mem_v, p, bias):
    n_streams, stream_rows, _ = x.shape
    n_rows = n_streams * stream_rows
    x2d = x.reshape(n_rows, D_MODEL)
    a_in = (x2d, state8, p['w_in'], p['sgu_g'], p['sgu_b'], p['w_s'], bias, p['w_conv'], p['w_out'],
            p['ln1g'], p['ln1b'], p['w_q'])
    x1, q, vn, conv8 = pl.pallas_call(
        functools.partial(_sample_a_kernel, n_rows=n_rows, stream_rows=stream_rows),
        grid=(1,),
        in_specs=[_const_spec(a.shape) for a in a_in],
        out_specs=[_whole_spec((n_rows, D_MODEL)), _whole_spec((n_rows, D_MODEL)),
                   _whole_spec((n_rows, SGU_WIDTH)), _whole_spec((n_streams, SUBLANES, CONV_WIDTH))],
        out_shape=[jax.ShapeDtypeStruct((n_rows, D_MODEL), F32),
                   jax.ShapeDtypeStruct((n_rows, D_MODEL), BF16),
                   jax.ShapeDtypeStruct((n_rows, SGU_WIDTH), F32),
                   jax.ShapeDtypeStruct((n_streams, SUBLANES, CONV_WIDTH), F32)],
        scratch_shapes=[pltpu.VMEM((n_rows, IN_PROJ), F32), pltpu.VMEM((n_rows, D_MODEL), BF16)],
        compiler_params=pltpu.CompilerParams(dimension_semantics=("arbitrary",),
                                             vmem_limit_bytes=VMEM_LIMIT_BYTES),
        name="sample_mixers",
    )(*a_in)

    qspec = pl.BlockSpec((stream_rows, D_MODEL), lambda s: (s, 0))
    kvspec = pl.BlockSpec((None, N_MEM, D_MODEL), lambda s: (s, 0, 0))
    o = pl.pallas_call(
        _sample_attn_kernel, grid=(n_streams,), in_specs=[qspec, kvspec, kvspec], out_specs=qspec,
        out_shape=jax.ShapeDtypeStruct((n_rows, D_MODEL), BF16),
        compiler_params=pltpu.CompilerParams(dimension_semantics=("arbitrary",)),
        name="sample_attention",
    )(q, mem_k, mem_v)

    c_in = (x1, o, p['w_o'], p['ln2g'], p['ln2b'], p['w_gate'], p['w_up'], p['w_down'],
            p['ln3g'], p['ln3b'])
    y = pl.pallas_call(
        _sample_c_kernel, grid=(1,),
        in_specs=[_const_spec(a.shape) for a in c_in],
        out_specs=_whole_spec((n_rows, D_MODEL)),
        out_shape=jax.ShapeDtypeStruct((n_rows, D_MODEL), F32),
        scratch_shapes=[pltpu.VMEM((n_rows, D_MODEL), BF16), pltpu.VMEM((n_rows, D_FF), BF16),
                        pltpu.VMEM((n_rows, D_MODEL), F32)],
        compiler_params=pltpu.CompilerParams(dimension_semantics=("arbitrary",),
                                             vmem_limit_bytes=VMEM_LIMIT_BYTES),
        name="sample_ffn",
    )(*c_in)
    return y.reshape(x.shape), conv8, vn


def _row_bias(b_s, t):
    b = jnp.tile(b_s[:, :t], (1, SGU_CHUNK // t))
    return jnp.repeat(b.T, SGU_HEAD_DIM, axis=1)


def kernel(x_prompt, x_sample, cache_mem_k, cache_mem_v, state_conv, mem_prompt, w_in, sgu_ln_g, sgu_ln_b, w_s, b_s, w_conv, w_out, ln_mix_g, ln_mix_b, w_q, w_mem_k, w_mem_v, w_mem_o, ln_mem_g, ln_mem_b, w_gate, w_up, w_down, ln_ffn_g, ln_ffn_b):
    depth = w_in.shape[0]
    assert depth == 1
    bsz, seq, _ = x_prompt.shape
    n_streams, dec_seq, _ = x_sample.shape
    assert seq % TILE_ROWS == 0 and SGU_CHUNK % dec_seq == 0 and dec_seq % SUBLANES == 0
    assert (n_streams * dec_seq) % SGU_CHUNK == 0
    l = 0
    row = lambda a: a.reshape(1, -1)
    p = dict(w_in=w_in[l].astype(BF16), sgu_g=sgu_ln_g[l], sgu_b=sgu_ln_b[l], w_s=w_s[l],
             w_conv=w_conv[l], w_out=w_out[l].astype(BF16), ln1g=row(ln_mix_g[l]), ln1b=row(ln_mix_b[l]),
             w_q=w_q[l].astype(BF16), w_o=w_mem_o[l].astype(BF16), ln2g=row(ln_mem_g[l]),
             ln2b=row(ln_mem_b[l]), w_gate=w_gate[l].astype(BF16), w_up=w_up[l].astype(BF16),
             w_down=w_down[l].astype(BF16), ln3g=row(ln_ffn_g[l]), ln3b=row(ln_ffn_b[l]))

    mk, mv = _mem_kv(mem_prompt, w_mem_k[l].astype(BF16), w_mem_v[l].astype(BF16))
    y_prompt, conv8_p = _prompt_layer(x_prompt, mk, mv, dict(p, bias=_row_bias(b_s[l], SGU_CHUNK)))

    state8 = jnp.pad(state_conv[l], ((0, 0), (SUBLANES - (CONV_K - 1), 0), (0, 0)))
    y_sample, conv8_s, vn = _sample_layer(x_sample, state8,
                                          cache_mem_k[l].reshape(n_streams, N_MEM, D_MODEL),
                                          cache_mem_v[l].reshape(n_streams, N_MEM, D_MODEL),
                                          p, _row_bias(b_s[l], dec_seq))

    tail = slice(SUBLANES - (CONV_K - 1), SUBLANES)
    return (y_prompt, y_sample,
            mk.reshape(depth, bsz, N_MEM, N_MEM_HEADS, MEM_HEAD_DIM),
            mv.reshape(depth, bsz, N_MEM, N_MEM_HEADS, MEM_HEAD_DIM),
            conv8_p[None, :, tail], conv8_s[None, :, tail],
            vn.reshape(depth, n_streams, dec_seq, N_SGU_HEADS, SGU_HEAD_DIM))
```

```python
import functools

import jax
import jax.numpy as jnp
from jax import lax
from jax.experimental import pallas as pl
from jax.experimental.pallas import tpu as pltpu

D_MODEL = 1024
CHUNK = 64
SGU_CHUNK = 128
N_SGU_HEADS = 4
SGU_WIDTH = D_MODEL // 2
SGU_HEAD_DIM = SGU_WIDTH // N_SGU_HEADS
CONV_WIDTH = D_MODEL // 2
CONV_K = 3
IN_PROJ = 2 * SGU_WIDTH + 3 * CONV_WIDTH
N_MEM = 256
N_MEM_HEADS = 4
MEM_HEAD_DIM = D_MODEL // N_MEM_HEADS
D_FF = 2816
ALPHA = 2.0 ** 0.25
LN_EPS = 1e-5
ATTN_SCALE = MEM_HEAD_DIM ** -0.5

SUBLANES = 8
MXU_WIDTH = 256
TILE_ROWS = 512
BLOCK_ROWS = 256
LN_ROWS = 16
SOFTMAX_ROWS = 64
VMEM_LIMIT_BYTES = 60 * 1024 * 1024

F32 = jnp.float32
BF16 = jnp.bfloat16

_U0, _V0, _GB0, _GC0, _XI0 = 0, SGU_WIDTH, 2 * SGU_WIDTH, 2 * SGU_WIDTH + CONV_WIDTH, 2 * SGU_WIDTH + 2 * CONV_WIDTH


def _dot(a, b):
    return jnp.dot(a, b, preferred_element_type=F32)


def _layernorm(x, g, b):
    mu = jnp.mean(x, axis=-1, keepdims=True)
    xc = x - mu
    var = jnp.mean(xc * xc, axis=-1, keepdims=True)
    return xc * lax.rsqrt(var + LN_EPS) * g + b


def _layernorm_rows(ref, rows, g_ref, b_ref, bf16_ref=None):
    for r in range(rows.start, rows.stop, LN_ROWS):
        piece = slice(r, r + LN_ROWS)
        y = _layernorm(ref[piece, :], g_ref[...], b_ref[...])
        ref[piece, :] = y
        if bf16_ref is not None:
            bf16_ref[piece, :] = y.astype(BF16)


def _shift_rows(cur, prev8, k):
    rolled = pltpu.roll(cur, k, axis=0)
    head = pltpu.roll(prev8, k, axis=0)
    row = lax.broadcasted_iota(jnp.int32, (SUBLANES, cur.shape[1]), 0)
    first = jnp.where(row < k, head, rolled[:SUBLANES])
    return jnp.concatenate([first, rolled[SUBLANES:]], axis=0)


def _sgu_norm(z_ref, sgu_g_ref, sgu_b_ref, rows):
    parts = []
    for h in range(N_SGU_HEADS):
        c0 = _V0 + h * SGU_HEAD_DIM
        parts.append(_layernorm(z_ref[rows, c0:c0 + SGU_HEAD_DIM],
                                sgu_g_ref[h:h + 1, :], sgu_b_ref[h:h + 1, :]))
    return parts


def _head_2d(ref, h):
    return ref[:, h * MEM_HEAD_DIM:(h + 1) * MEM_HEAD_DIM].astype(BF16)


def _attn_scores(q_ref, k_ref, rows, head):
    scores = []
    for h in range(N_MEM_HEADS):
        c = slice(h * MEM_HEAD_DIM, (h + 1) * MEM_HEAD_DIM)
        scores.append(lax.dot_general(q_ref[rows, c], head(k_ref, h),
                                      (((1,), (1,)), ((), ())), preferred_element_type=F32) * ATTN_SCALE)
    return scores


def _attn_values(scores, v_ref, o_ref, rows, head):
    for h, sc in enumerate(scores):
        c = slice(h * MEM_HEAD_DIM, (h + 1) * MEM_HEAD_DIM)
        pieces = []
        for r in range(0, sc.shape[0], SOFTMAX_ROWS):
            s = sc[r:r + SOFTMAX_ROWS]
            e = jnp.exp(s - jnp.max(s, axis=-1, keepdims=True))
            pieces.append((e / jnp.sum(e, axis=-1, keepdims=True)).astype(BF16))
        p = jnp.concatenate(pieces, axis=0) if len(pieces) > 1 else pieces[0]
        o_ref[rows, c] = _dot(p, head(v_ref, h)).astype(BF16)


def _ffn(x2_ref, xb_ref, w_gate_ref, w_up_ref, w_down_ref, g_ref, b_ref, h_ref, y_ref, rows):
    for j in range(D_FF // MXU_WIDTH):
        c = slice(j * MXU_WIDTH, (j + 1) * MXU_WIDTH)
        gate = _dot(xb_ref[rows, :], w_gate_ref[:, c])
        up = _dot(xb_ref[rows, :], w_up_ref[:, c])
        h_ref[rows, c] = (gate * jax.nn.sigmoid(gate) * up).astype(BF16)
    k1 = D_FF - 2 * MXU_WIDTH
    f = _dot(h_ref[rows, :k1], w_down_ref[:k1, :]) + _dot(h_ref[rows, k1:], w_down_ref[k1:, :])
    y_ref[rows, :] = ALPHA * x2_ref[rows, :] + f
    _layernorm_rows(y_ref, rows, g_ref, b_ref)


def _masked_mix_weights(w_s_ref, t):
    row = lax.broadcasted_iota(jnp.int32, (SGU_CHUNK, SGU_CHUNK), 0)
    col = lax.broadcasted_iota(jnp.int32, (SGU_CHUNK, SGU_CHUNK), 1)
    chunk_of = lambda i: lax.shift_right_logical(i, CHUNK.bit_length() - 1)
    out = []
    for h in range(N_SGU_HEADS):
        w = w_s_ref[h]
        if t == SGU_CHUNK:
            keep = chunk_of(col) <= chunk_of(row)
        else:
            w = jnp.where(col < t, jnp.concatenate([w[:t]] * (SGU_CHUNK // t), axis=0), 0.0)
            acc = w
            for r in range(1, SGU_CHUNK // t):
                acc = acc + pltpu.roll(w, r * t, axis=1)
            w = acc
            block_of = lambda i: lax.shift_right_logical(i, t.bit_length() - 1)
            within = lambda i: i & (t - 1)
            keep = (block_of(col) == block_of(row)) & (chunk_of(within(col)) <= chunk_of(within(row)))
        out.append(jnp.where(keep, w, 0.0).astype(BF16))
    return out


def _mixers(z_ref, wm, bias_ref, w_conv_ref, vn_fn, prev8_fn, mix_ref, row0, n_rows, stream_rows):
    cx_prev = None
    for c in range(n_rows // SGU_CHUNK):
        rows = slice(row0 + c * SGU_CHUNK, row0 + (c + 1) * SGU_CHUNK)
        vn = vn_fn(rows)
        for h in range(N_SGU_HEADS):
            cols = slice(h * SGU_HEAD_DIM, (h + 1) * SGU_HEAD_DIM)
            mixed = _dot(wm[h], vn[h].astype(BF16)) + bias_ref[:, cols]
            u = z_ref[rows, _U0 + h * SGU_HEAD_DIM:_U0 + (h + 1) * SGU_HEAD_DIM]
            mix_ref[rows, cols] = (u * mixed).astype(BF16)
        cx = z_ref[rows, _GC0:_GC0 + CONV_WIDTH] * z_ref[rows, _XI0:_XI0 + CONV_WIDTH]
        if stream_rows >= SGU_CHUNK:
            prev8 = (prev8_fn((c * SGU_CHUNK) // stream_rows) if (c * SGU_CHUNK) % stream_rows == 0
                     else cx_prev[-SUBLANES:])
            s1 = _shift_rows(cx, prev8, 1)
            s2 = _shift_rows(cx, prev8, 2)
        else:
            s1p, s2p = [], []
            for s in range(SGU_CHUNK // stream_rows):
                blk = cx[s * stream_rows:(s + 1) * stream_rows]
                prev8 = prev8_fn(c * (SGU_CHUNK // stream_rows) + s)
                s1p.append(_shift_rows(blk, prev8, 1))
                s2p.append(_shift_rows(blk, prev8, 2))
            s1 = jnp.concatenate(s1p, axis=0)
            s2 = jnp.concatenate(s2p, axis=0)
        conv = s2 * w_conv_ref[0:1, :] + s1 * w_conv_ref[1:2, :] + cx * w_conv_ref[2:3, :]
        mix_ref[rows, SGU_WIDTH:] = (z_ref[rows, _GB0:_GB0 + CONV_WIDTH] * conv).astype(BF16)
        cx_prev = cx
    return cx_prev[-SUBLANES:]


def _prompt_kernel(x_ref, mk_ref, mv_ref, w_in_ref, sgu_g_ref, sgu_b_ref, w_s_ref, bias_ref,
                   w_conv_ref, w_out_ref, ln1g_ref, ln1b_ref, w_q_ref, w_o_ref, ln2g_ref, ln2b_ref,
                   w_gate_ref, w_up_ref, w_down_ref, ln3g_ref, ln3b_ref,
                   y_ref, conv_ref,
                   z_ref, act_a, act_b, h_ref, res_ref, carry_ref):
    @pl.when(pl.program_id(1) == 0)
    def _():
        carry_ref[...] = jnp.zeros_like(carry_ref)

    wm = _masked_mix_weights(w_s_ref, SGU_CHUNK)
    blocks = [slice(i * BLOCK_ROWS, (i + 1) * BLOCK_ROWS) for i in range(TILE_ROWS // BLOCK_ROWS)]
    prev8 = carry_ref[...]
    for rows in blocks:
        xb = x_ref[rows, :].astype(BF16)
        for cols in (slice(_V0, _GB0), slice(_U0, _V0), slice(_GB0, IN_PROJ)):
            z_ref[rows, cols] = _dot(xb, w_in_ref[:, cols])
        prev8 = _mixers(z_ref, wm, bias_ref, w_conv_ref,
                        lambda r: _sgu_norm(z_ref, sgu_g_ref, sgu_b_ref, r),
                        lambda _, prev8=prev8: prev8, act_a, rows.start, BLOCK_ROWS, BLOCK_ROWS)
    carry_ref[...] = prev8
    conv_ref[...] = prev8
    for rows in blocks:
        res_ref[rows, :] = ALPHA * x_ref[rows, :] + _dot(act_a[rows, :], w_out_ref[...])
        _layernorm_rows(res_ref, rows, ln1g_ref, ln1b_ref, act_b)
    scores = []
    for rows in blocks:
        act_a[rows, :] = _dot(act_b[rows, :], w_q_ref[...]).astype(BF16)
        scores.append(_attn_scores(act_a, mk_ref, rows, _head_2d))
    for rows, sc in zip(blocks, scores):
        _attn_values(sc, mv_ref, act_b, rows, _head_2d)
        res_ref[rows, :] = ALPHA * res_ref[rows, :] + _dot(act_b[rows, :], w_o_ref[...])
        _layernorm_rows(res_ref, rows, ln2g_ref, ln2b_ref, act_a)
    for rows in blocks:
        _ffn(res_ref, act_a, w_gate_ref, w_up_ref, w_down_ref, ln3g_ref, ln3b_ref, h_ref, y_ref, rows)


def _const_spec(shape):
    return pl.BlockSpec(shape, lambda *_: (0,) * len(shape), pipeline_mode=pl.Buffered(1))


def _whole_spec(shape):
    return pl.BlockSpec(shape, lambda *_: (0,) * len(shape))


def _prompt_layer(x, mk, mv, p):
    bsz, seq, _ = x.shape
    n_tiles = seq // TILE_ROWS
    weights = (p['w_in'], p['sgu_g'], p['sgu_b'], p['w_s'], p['bias'], p['w_conv'], p['w_out'],
               p['ln1g'], p['ln1b'], p['w_q'], p['w_o'], p['ln2g'], p['ln2b'],
               p['w_gate'], p['w_up'], p['w_down'], p['ln3g'], p['ln3b'])
    kv_spec = pl.BlockSpec((None, N_MEM, D_MODEL), lambda b, s: (b, 0, 0))
    return pl.pallas_call(
        _prompt_kernel,
        grid=(bsz, n_tiles),
        in_specs=[pl.BlockSpec((None, TILE_ROWS, D_MODEL), lambda b, s: (b, s, 0)), kv_spec, kv_spec]
        + [_const_spec(w.shape) for w in weights],
        out_specs=[pl.BlockSpec((None, TILE_ROWS, D_MODEL), lambda b, s: (b, s, 0)),
                   pl.BlockSpec((None, SUBLANES, CONV_WIDTH), lambda b, s: (b, 0, 0))],
        out_shape=[jax.ShapeDtypeStruct((bsz, seq, D_MODEL), F32),
                   jax.ShapeDtypeStruct((bsz, SUBLANES, CONV_WIDTH), F32)],
        scratch_shapes=[pltpu.VMEM((TILE_ROWS, IN_PROJ), F32),
                        pltpu.VMEM((TILE_ROWS, D_MODEL), BF16),
                        pltpu.VMEM((TILE_ROWS, D_MODEL), BF16),
                        pltpu.VMEM((TILE_ROWS, D_FF), BF16),
                        pltpu.VMEM((TILE_ROWS, D_MODEL), F32),
                        pltpu.VMEM((SUBLANES, CONV_WIDTH), F32)],
        compiler_params=pltpu.CompilerParams(dimension_semantics=("arbitrary", "arbitrary"),
                                             vmem_limit_bytes=VMEM_LIMIT_BYTES),
        name="prompt_layer",
    )(x, mk, mv, *weights)


def _mem_kv_kernel(m_ref, wk_ref, wv_ref, k_ref, v_ref, kb_ref, vb_ref):
    m = m_ref[...].astype(BF16)
    for w_ref, o_ref, ob_ref in ((wk_ref, k_ref, kb_ref), (wv_ref, v_ref, vb_ref)):
        kv = _dot(m, w_ref[...])
        ob_ref[...] = kv.astype(BF16)
        for h in range(N_MEM_HEADS):
            o_ref[:, h, :] = kv[:, h * MEM_HEAD_DIM:(h + 1) * MEM_HEAD_DIM]


def _mem_kv(mem, wk, wv):
    bsz = mem.shape[0]
    spec = pl.BlockSpec((None, N_MEM, D_MODEL), lambda b: (b, 0, 0))
    hspec = pl.BlockSpec((None, N_MEM, N_MEM_HEADS, MEM_HEAD_DIM), lambda b: (b, 0, 0, 0))
    wspec = pl.BlockSpec((D_MODEL, D_MODEL), lambda b: (0, 0))
    return pl.pallas_call(
        _mem_kv_kernel, grid=(bsz,), in_specs=[spec, wspec, wspec], out_specs=[hspec, hspec, spec, spec],
        out_shape=[jax.ShapeDtypeStruct((bsz, N_MEM, N_MEM_HEADS, MEM_HEAD_DIM), F32)] * 2
        + [jax.ShapeDtypeStruct((bsz, N_MEM, D_MODEL), BF16)] * 2,
        compiler_params=pltpu.CompilerParams(dimension_semantics=("arbitrary",)),
        name="mem_kv",
    )(mem, wk, wv)


def _sample_a_kernel(x_ref, st_ref, w_in_ref, sgu_g_ref, sgu_b_ref, w_s_ref, bias_ref, w_conv_ref,
                     w_out_ref, ln1g_ref, ln1b_ref, w_q_ref,
                     x1_ref, q_ref, vn_ref, conv_ref, z_ref, mix_ref, *, n_rows, stream_rows):
    x = x_ref[...]
    z_ref[...] = _dot(x.astype(BF16), w_in_ref[...])
    wm = _masked_mix_weights(w_s_ref, stream_rows)

    def vn_fn(rows):
        parts = _sgu_norm(z_ref, sgu_g_ref, sgu_b_ref, rows)
        for h in range(N_SGU_HEADS):
            vn_ref[rows, h * SGU_HEAD_DIM:(h + 1) * SGU_HEAD_DIM] = parts[h]
        return parts

    def prev8_fn(s):
        return st_ref[s]

    _mixers(z_ref, wm, bias_ref, w_conv_ref, vn_fn, prev8_fn, mix_ref, 0, n_rows, stream_rows)
    for s in range(n_rows // stream_rows):
        r0 = (s + 1) * stream_rows - SUBLANES
        conv_ref[s] = (z_ref[r0:r0 + SUBLANES, _GC0:_GC0 + CONV_WIDTH]
                       * z_ref[r0:r0 + SUBLANES, _XI0:_XI0 + CONV_WIDTH])
    x1 = _layernorm(ALPHA * x + _dot(mix_ref[...], w_out_ref[...]), ln1g_ref[...], ln1b_ref[...])
    x1_ref[...] = x1
    q_ref[...] = _dot(x1.astype(BF16), w_q_ref[...]).astype(BF16)


def _sample_attn_kernel(q_ref, k_ref, v_ref, o_ref):
    _attn_values(_attn_scores(q_ref, k_ref, slice(None), _head_2d), v_ref, o_ref, slice(None), _head_2d)


def _sample_c_kernel(x1_ref, o_ref, w_o_ref, ln2g_ref, ln2b_ref, w_gate_ref, w_up_ref, w_down_ref,
                     ln3g_ref, ln3b_ref, y_ref, xb_ref, h_ref, res_ref):
    x2 = _layernorm(ALPHA * x1_ref[...] + _dot(o_ref[...], w_o_ref[...]), ln2g_ref[...], ln2b_ref[...])
    res_ref[...] = x2
    xb_ref[...] = x2.astype(BF16)
    _ffn(res_ref, xb_ref, w_gate_ref, w_up_ref, w_down_ref, ln3g_ref, ln3b_ref, h_ref, y_ref,
         slice(0, y_ref.shape[0]))


def _sample_layer(x, state8, mem_k, mem_v, p, bias):
    n_streams, stream_rows, _ = x.shape
    n_rows = n_streams * stream_rows
    x2d = x.reshape(n_rows, D_MODEL)
    a_in = (x2d, state8, p['w_in'], p['sgu_g'], p['sgu_b'], p['w_s'], bias, p['w_conv'], p['w_out'],
            p['ln1g'], p['ln1b'], p['w_q'])
    x1, q, vn, conv8 = pl.pallas_call(
        functools.partial(_sample_a_kernel, n_rows=n_rows, stream_rows=stream_rows),
        grid=(1,),
        in_specs=[_const_spec(a.shape) for a in a_in],
        out_specs=[_whole_spec((n_rows, D_MODEL)), _whole_spec((n_rows, D_MODEL)),
                   _whole_spec((n_rows, SGU_WIDTH)), _whole_spec((n_streams, SUBLANES, CONV_WIDTH))],
        out_shape=[jax.ShapeDtypeStruct((n_rows, D_MODEL), F32),
                   jax.ShapeDtypeStruct((n_rows, D_MODEL), BF16),
                   jax.ShapeDtypeStruct((n_rows, SGU_WIDTH), F32),
                   jax.ShapeDtypeStruct((n_streams, SUBLANES, CONV_WIDTH), F32)],
        scratch_shapes=[pltpu.VMEM((n_rows, IN_PROJ), F32), pltpu.VMEM((n_rows, D_MODEL), BF16)],
        compiler_params=pltpu.CompilerParams(dimension_semantics=("arbitrary",),
                                             vmem_limit_bytes=VMEM_LIMIT_BYTES),
        name="sample_mixers",
    )(*a_in)

    qspec = pl.BlockSpec((stream_rows, D_MODEL), lambda s: (s, 0))
    kvspec = pl.BlockSpec((None, N_MEM, D_MODEL), lambda s: (s, 0, 0))
    o = pl.pallas_call(
        _sample_attn_kernel, grid=(n_streams,), in_specs=[qspec, kvspec, kvspec], out_specs=qspec,
        out_shape=jax.ShapeDtypeStruct((n_rows, D_MODEL), BF16),
        compiler_params=pltpu.CompilerParams(dimension_semantics=("arbitrary",)),
        name="sample_attention",
    )(q, mem_k.reshape(n_streams, N_MEM, D_MODEL), mem_v.reshape(n_streams, N_MEM, D_MODEL))

    c_in = (x1, o, p['w_o'], p['ln2g'], p['ln2b'], p['w_gate'], p['w_up'], p['w_down'],
            p['ln3g'], p['ln3b'])
    y = pl.pallas_call(
        _sample_c_kernel, grid=(1,),
        in_specs=[_const_spec(a.shape) for a in c_in],
        out_specs=_whole_spec((n_rows, D_MODEL)),
        out_shape=jax.ShapeDtypeStruct((n_rows, D_MODEL), F32),
        scratch_shapes=[pltpu.VMEM((n_rows, D_MODEL), BF16), pltpu.VMEM((n_rows, D_FF), BF16),
                        pltpu.VMEM((n_rows, D_MODEL), F32)],
        compiler_params=pltpu.CompilerParams(dimension_semantics=("arbitrary",),
                                             vmem_limit_bytes=VMEM_LIMIT_BYTES),
        name="sample_ffn",
    )(*c_in)
    return y.reshape(x.shape), conv8, vn


def _row_bias(b_s, t):
    b = jnp.tile(b_s[:, :t], (1, SGU_CHUNK // t))
    return jnp.repeat(b.T, SGU_HEAD_DIM, axis=1)


def kernel(x_prompt, x_sample, cache_mem_k, cache_mem_v, state_conv, mem_prompt, w_in, sgu_ln_g, sgu_ln_b, w_s, b_s, w_conv, w_out, ln_mix_g, ln_mix_b, w_q, w_mem_k, w_mem_v, w_mem_o, ln_mem_g, ln_mem_b, w_gate, w_up, w_down, ln_ffn_g, ln_ffn_b):
    depth = w_in.shape[0]
    assert depth == 1
    bsz, seq, _ = x_prompt.shape
    n_streams, dec_seq, _ = x_sample.shape
    assert seq % TILE_ROWS == 0 and SGU_CHUNK % dec_seq == 0 and dec_seq % SUBLANES == 0
    assert (n_streams * dec_seq) % SGU_CHUNK == 0
    l = 0
    row = lambda a: a.reshape(1, -1)
    p = dict(w_in=w_in[l].astype(BF16), sgu_g=sgu_ln_g[l], sgu_b=sgu_ln_b[l], w_s=w_s[l],
             w_conv=w_conv[l], w_out=w_out[l].astype(BF16), ln1g=row(ln_mix_g[l]), ln1b=row(ln_mix_b[l]),
             w_q=w_q[l].astype(BF16), w_o=w_mem_o[l].astype(BF16), ln2g=row(ln_mem_g[l]),
             ln2b=row(ln_mem_b[l]), w_gate=w_gate[l].astype(BF16), w_up=w_up[l].astype(BF16),
             w_down=w_down[l].astype(BF16), ln3g=row(ln_ffn_g[l]), ln3b=row(ln_ffn_b[l]))

    mk, mv, mk_flat, mv_flat = _mem_kv(mem_prompt, w_mem_k[l].astype(BF16), w_mem_v[l].astype(BF16))
    y_prompt, conv8_p = _prompt_layer(x_prompt, mk_flat, mv_flat,
                                      dict(p, bias=_row_bias(b_s[l], SGU_CHUNK)))

    state8 = jnp.pad(state_conv[l], ((0, 0), (SUBLANES - (CONV_K - 1), 0), (0, 0)))
    y_sample, conv8_s, vn = _sample_layer(x_sample, state8, cache_mem_k[l], cache_mem_v[l],
                                          p, _row_bias(b_s[l], dec_seq))

    tail = slice(SUBLANES - (CONV_K - 1), SUBLANES)
    return (y_prompt, y_sample, mk[None], mv[None],
            conv8_p[None, :, tail], conv8_s[None, :, tail],
            vn.reshape(depth, n_streams, dec_seq, N_SGU_HEADS, SGU_HEAD_DIM))
```

```python
import functools

import jax
import jax.numpy as jnp
from jax import lax
from jax.experimental import pallas as pl
from jax.experimental.pallas import tpu as pltpu

D_MODEL = 1024
CHUNK = 64
SGU_CHUNK = 128
N_SGU_HEADS = 4
SGU_WIDTH = D_MODEL // 2
SGU_HEAD_DIM = SGU_WIDTH // N_SGU_HEADS
CONV_WIDTH = D_MODEL // 2
CONV_K = 3
IN_PROJ = 2 * SGU_WIDTH + 3 * CONV_WIDTH
N_MEM = 256
N_MEM_HEADS = 4
MEM_HEAD_DIM = D_MODEL // N_MEM_HEADS
D_FF = 2816
ALPHA = 2.0 ** 0.25
LN_EPS = 1e-5
ATTN_SCALE = MEM_HEAD_DIM ** -0.5

SUBLANES = 8
MXU_WIDTH = 256
TILE_ROWS = 1024
PAIR_ROWS = 512
BLOCK_ROWS = 256
LN_ROWS = 16
SOFTMAX_ROWS = 64
VMEM_LIMIT_BYTES = 60 * 1024 * 1024

F32 = jnp.float32
BF16 = jnp.bfloat16

_U0, _V0, _GB0, _GC0, _XI0 = 0, SGU_WIDTH, 2 * SGU_WIDTH, 2 * SGU_WIDTH + CONV_WIDTH, 2 * SGU_WIDTH + 2 * CONV_WIDTH


def _dot(a, b):
    return jnp.dot(a, b, preferred_element_type=F32)


def _layernorm(x, g, b):
    mu = jnp.mean(x, axis=-1, keepdims=True)
    xc = x - mu
    var = jnp.mean(xc * xc, axis=-1, keepdims=True)
    return xc * lax.rsqrt(var + LN_EPS) * g + b


def _layernorm_rows(ref, rows, g_ref, b_ref, bf16_ref=None):
    for r in range(rows.start, rows.stop, LN_ROWS):
        piece = slice(r, r + LN_ROWS)
        y = _layernorm(ref[piece, :], g_ref[...], b_ref[...])
        ref[piece, :] = y
        if bf16_ref is not None:
            bf16_ref[piece, :] = y.astype(BF16)


def _shift_rows(cur, prev8, k):
    rolled = pltpu.roll(cur, k, axis=0)
    head = pltpu.roll(prev8, k, axis=0)
    row = lax.broadcasted_iota(jnp.int32, (SUBLANES, cur.shape[1]), 0)
    first = jnp.where(row < k, head, rolled[:SUBLANES])
    return jnp.concatenate([first, rolled[SUBLANES:]], axis=0)


def _sgu_norm(z_ref, sgu_g_ref, sgu_b_ref, rows):
    parts = []
    for h in range(N_SGU_HEADS):
        c0 = _V0 + h * SGU_HEAD_DIM
        parts.append(_layernorm(z_ref[rows, c0:c0 + SGU_HEAD_DIM],
                                sgu_g_ref[h:h + 1, :], sgu_b_ref[h:h + 1, :]))
    return parts


def _head_2d(ref, h):
    return ref[:, h * MEM_HEAD_DIM:(h + 1) * MEM_HEAD_DIM].astype(BF16)


def _attn_scores(q_ref, k_ref, rows, head):
    scores = []
    for h in range(N_MEM_HEADS):
        c = slice(h * MEM_HEAD_DIM, (h + 1) * MEM_HEAD_DIM)
        scores.append(lax.dot_general(q_ref[rows, c], head(k_ref, h),
                                      (((1,), (1,)), ((), ())), preferred_element_type=F32) * ATTN_SCALE)
    return scores


def _attn_values(scores, v_ref, o_ref, rows, head):
    for h, sc in enumerate(scores):
        c = slice(h * MEM_HEAD_DIM, (h + 1) * MEM_HEAD_DIM)
        pieces = []
        for r in range(0, sc.shape[0], SOFTMAX_ROWS):
            s = sc[r:r + SOFTMAX_ROWS]
            e = jnp.exp(s - jnp.max(s, axis=-1, keepdims=True))
            pieces.append((e / jnp.sum(e, axis=-1, keepdims=True)).astype(BF16))
        p = jnp.concatenate(pieces, axis=0) if len(pieces) > 1 else pieces[0]
        o_ref[rows, c] = _dot(p, head(v_ref, h)).astype(BF16)


def _ffn(x2_ref, xb_ref, w_gate_ref, w_up_ref, w_down_ref, g_ref, b_ref, h_ref, y_ref, rows):
    for j in range(D_FF // MXU_WIDTH):
        c = slice(j * MXU_WIDTH, (j + 1) * MXU_WIDTH)
        gate = _dot(xb_ref[rows, :], w_gate_ref[:, c])
        up = _dot(xb_ref[rows, :], w_up_ref[:, c])
        h_ref[rows, c] = (gate * jax.nn.sigmoid(gate) * up).astype(BF16)
    k1 = D_FF - 2 * MXU_WIDTH
    f = _dot(h_ref[rows, :k1], w_down_ref[:k1, :]) + _dot(h_ref[rows, k1:], w_down_ref[k1:, :])
    y_ref[rows, :] = ALPHA * x2_ref[rows, :] + f
    _layernorm_rows(y_ref, rows, g_ref, b_ref)


def _masked_mix_weights(w_s_ref, t):
    row = lax.broadcasted_iota(jnp.int32, (SGU_CHUNK, SGU_CHUNK), 0)
    col = lax.broadcasted_iota(jnp.int32, (SGU_CHUNK, SGU_CHUNK), 1)
    chunk_of = lambda i: lax.shift_right_logical(i, CHUNK.bit_length() - 1)
    out = []
    for h in range(N_SGU_HEADS):
        w = w_s_ref[h]
        if t == SGU_CHUNK:
            keep = chunk_of(col) <= chunk_of(row)
        else:
            w = jnp.where(col < t, jnp.concatenate([w[:t]] * (SGU_CHUNK // t), axis=0), 0.0)
            acc = w
            for r in range(1, SGU_CHUNK // t):
                acc = acc + pltpu.roll(w, r * t, axis=1)
            w = acc
            block_of = lambda i: lax.shift_right_logical(i, t.bit_length() - 1)
            within = lambda i: i & (t - 1)
            keep = (block_of(col) == block_of(row)) & (chunk_of(within(col)) <= chunk_of(within(row)))
        out.append(jnp.where(keep, w, 0.0).astype(BF16))
    return out


def _mixers(z_ref, wm, bias_ref, w_conv_ref, vn_fn, prev8_fn, mix_ref, row0, n_rows, stream_rows):
    cx_prev = None
    for c in range(n_rows // SGU_CHUNK):
        rows = slice(row0 + c * SGU_CHUNK, row0 + (c + 1) * SGU_CHUNK)
        vn = vn_fn(rows)
        for h in range(N_SGU_HEADS):
            cols = slice(h * SGU_HEAD_DIM, (h + 1) * SGU_HEAD_DIM)
            mixed = _dot(wm[h], vn[h].astype(BF16)) + bias_ref[:, cols]
            u = z_ref[rows, _U0 + h * SGU_HEAD_DIM:_U0 + (h + 1) * SGU_HEAD_DIM]
            mix_ref[rows, cols] = (u * mixed).astype(BF16)
        cx = z_ref[rows, _GC0:_GC0 + CONV_WIDTH] * z_ref[rows, _XI0:_XI0 + CONV_WIDTH]
        if stream_rows >= SGU_CHUNK:
            prev8 = (prev8_fn((c * SGU_CHUNK) // stream_rows) if (c * SGU_CHUNK) % stream_rows == 0
                     else cx_prev[-SUBLANES:])
            s1 = _shift_rows(cx, prev8, 1)
            s2 = _shift_rows(cx, prev8, 2)
        else:
            s1p, s2p = [], []
            for s in range(SGU_CHUNK // stream_rows):
                blk = cx[s * stream_rows:(s + 1) * stream_rows]
                prev8 = prev8_fn(c * (SGU_CHUNK // stream_rows) + s)
                s1p.append(_shift_rows(blk, prev8, 1))
                s2p.append(_shift_rows(blk, prev8, 2))
            s1 = jnp.concatenate(s1p, axis=0)
            s2 = jnp.concatenate(s2p, axis=0)
        conv = s2 * w_conv_ref[0:1, :] + s1 * w_conv_ref[1:2, :] + cx * w_conv_ref[2:3, :]
        mix_ref[rows, SGU_WIDTH:] = (z_ref[rows, _GB0:_GB0 + CONV_WIDTH] * conv).astype(BF16)
        cx_prev = cx
    return cx_prev[-SUBLANES:]


def _prompt_kernel(x_ref, mk_ref, mv_ref, w_in_ref, sgu_g_ref, sgu_b_ref, w_s_ref, bias_ref,
                   w_conv_ref, w_out_ref, ln1g_ref, ln1b_ref, w_q_ref, w_o_ref, ln2g_ref, ln2b_ref,
                   w_gate_ref, w_up_ref, w_down_ref, ln3g_ref, ln3b_ref,
                   y_ref, conv_ref,
                   z_ref, act_a, act_b, h_ref, res_ref, carry_ref):
    @pl.when(pl.program_id(1) == 0)
    def _():
        carry_ref[...] = jnp.zeros_like(carry_ref)

    wm = _masked_mix_weights(w_s_ref, SGU_CHUNK)
    blocks = [slice(i * BLOCK_ROWS, (i + 1) * BLOCK_ROWS) for i in range(PAIR_ROWS // BLOCK_ROWS)]
    prev8 = carry_ref[...]
    for pair in range(TILE_ROWS // PAIR_ROWS):
        x_view = x_ref.at[pl.ds(pair * PAIR_ROWS, PAIR_ROWS)]
        y_view = y_ref.at[pl.ds(pair * PAIR_ROWS, PAIR_ROWS)]
        prev8 = _prompt_pair(x_view, y_view, prev8, wm, blocks, mk_ref, mv_ref, w_in_ref, sgu_g_ref,
                             sgu_b_ref, bias_ref, w_conv_ref, w_out_ref, ln1g_ref, ln1b_ref, w_q_ref,
                             w_o_ref, ln2g_ref, ln2b_ref, w_gate_ref, w_up_ref, w_down_ref, ln3g_ref,
                             ln3b_ref, z_ref, act_a, act_b, h_ref, res_ref)
    carry_ref[...] = prev8
    conv_ref[...] = prev8


def _prompt_pair(x_ref, y_ref, prev8, wm, blocks, mk_ref, mv_ref, w_in_ref, sgu_g_ref, sgu_b_ref,
                 bias_ref, w_conv_ref, w_out_ref, ln1g_ref, ln1b_ref, w_q_ref, w_o_ref, ln2g_ref,
                 ln2b_ref, w_gate_ref, w_up_ref, w_down_ref, ln3g_ref, ln3b_ref,
                 z_ref, act_a, act_b, h_ref, res_ref):
    for rows in blocks:
        xb = x_ref[rows, :].astype(BF16)
        for cols in (slice(_V0, _GB0), slice(_U0, _V0), slice(_GB0, IN_PROJ)):
            z_ref[rows, cols] = _dot(xb, w_in_ref[:, cols])
        prev8 = _mixers(z_ref, wm, bias_ref, w_conv_ref,
                        lambda r: _sgu_norm(z_ref, sgu_g_ref, sgu_b_ref, r),
                        lambda _, prev8=prev8: prev8, act_a, rows.start, BLOCK_ROWS, BLOCK_ROWS)
    for rows in blocks:
        res_ref[rows, :] = ALPHA * x_ref[rows, :] + _dot(act_a[rows, :], w_out_ref[...])
        _layernorm_rows(res_ref, rows, ln1g_ref, ln1b_ref, act_b)
    scores = []
    for rows in blocks:
        act_a[rows, :] = _dot(act_b[rows, :], w_q_ref[...]).astype(BF16)
        scores.append(_attn_scores(act_a, mk_ref, rows, _head_2d))
    for rows, sc in zip(blocks, scores):
        _attn_values(sc, mv_ref, act_b, rows, _head_2d)
        res_ref[rows, :] = ALPHA * res_ref[rows, :] + _dot(act_b[rows, :], w_o_ref[...])
        _layernorm_rows(res_ref, rows, ln2g_ref, ln2b_ref, act_a)
    for rows in blocks:
        _ffn(res_ref, act_a, w_gate_ref, w_up_ref, w_down_ref, ln3g_ref, ln3b_ref, h_ref, y_ref, rows)
    return prev8


def _const_spec(shape):
    return pl.BlockSpec(shape, lambda *_: (0,) * len(shape), pipeline_mode=pl.Buffered(1))


def _whole_spec(shape):
    return pl.BlockSpec(shape, lambda *_: (0,) * len(shape))


def _prompt_layer(x, mk, mv, p):
    bsz, seq, _ = x.shape
    n_tiles = seq // TILE_ROWS
    weights = (p['w_in'], p['sgu_g'], p['sgu_b'], p['w_s'], p['bias'], p['w_conv'], p['w_out'],
               p['ln1g'], p['ln1b'], p['w_q'], p['w_o'], p['ln2g'], p['ln2b'],
               p['w_gate'], p['w_up'], p['w_down'], p['ln3g'], p['ln3b'])
    kv_spec = pl.BlockSpec((None, N_MEM, D_MODEL), lambda b, s: (b, 0, 0))
    return pl.pallas_call(
        _prompt_kernel,
        grid=(bsz, n_tiles),
        in_specs=[pl.BlockSpec((None, TILE_ROWS, D_MODEL), lambda b, s: (b, s, 0)), kv_spec, kv_spec]
        + [_const_spec(w.shape) for w in weights],
        out_specs=[pl.BlockSpec((None, TILE_ROWS, D_MODEL), lambda b, s: (b, s, 0)),
                   pl.BlockSpec((None, SUBLANES, CONV_WIDTH), lambda b, s: (b, 0, 0))],
        out_shape=[jax.ShapeDtypeStruct((bsz, seq, D_MODEL), F32),
                   jax.ShapeDtypeStruct((bsz, SUBLANES, CONV_WIDTH), F32)],
        scratch_shapes=[pltpu.VMEM((PAIR_ROWS, IN_PROJ), F32),
                        pltpu.VMEM((PAIR_ROWS, D_MODEL), BF16),
                        pltpu.VMEM((PAIR_ROWS, D_MODEL), BF16),
                        pltpu.VMEM((PAIR_ROWS, D_FF), BF16),
                        pltpu.VMEM((PAIR_ROWS, D_MODEL), F32),
                        pltpu.VMEM((SUBLANES, CONV_WIDTH), F32)],
        compiler_params=pltpu.CompilerParams(dimension_semantics=("arbitrary", "arbitrary"),
                                             vmem_limit_bytes=VMEM_LIMIT_BYTES),
        name="prompt_layer",
    )(x, mk, mv, *weights)


def _mem_kv_kernel(m_ref, wk_ref, wv_ref, k_ref, v_ref, kb_ref, vb_ref):
    m = m_ref[...].astype(BF16)
    for w_ref, o_ref, ob_ref in ((wk_ref, k_ref, kb_ref), (wv_ref, v_ref, vb_ref)):
        kv = _dot(m, w_ref[...])
        ob_ref[...] = kv.astype(BF16)
        for h in range(N_MEM_HEADS):
            o_ref[:, h, :] = kv[:, h * MEM_HEAD_DIM:(h + 1) * MEM_HEAD_DIM]


def _mem_kv(mem, wk, wv):
    bsz = mem.shape[0]
    spec = pl.BlockSpec((None, N_MEM, D_MODEL), lambda b: (b, 0, 0))
    hspec = pl.BlockSpec((None, N_MEM, N_MEM_HEADS, MEM_HEAD_DIM), lambda b: (b, 0, 0, 0))
    wspec = pl.BlockSpec((D_MODEL, D_MODEL), lambda b: (0, 0))
    return pl.pallas_call(
        _mem_kv_kernel, grid=(bsz,), in_specs=[spec, wspec, wspec], out_specs=[hspec, hspec, spec, spec],
        out_shape=[jax.ShapeDtypeStruct((bsz, N_MEM, N_MEM_HEADS, MEM_HEAD_DIM), F32)] * 2
        + [jax.ShapeDtypeStruct((bsz, N_MEM, D_MODEL), BF16)] * 2,
        compiler_params=pltpu.CompilerParams(dimension_semantics=("arbitrary",)),
        name="mem_kv",
    )(mem, wk, wv)


def _sample_a_kernel(x_ref, st_ref, w_in_ref, sgu_g_ref, sgu_b_ref, w_s_ref, bias_ref, w_conv_ref,
                     w_out_ref, ln1g_ref, ln1b_ref, w_q_ref,
                     x1_ref, q_ref, vn_ref, conv_ref, z_ref, mix_ref, *, n_rows, stream_rows):
    x = x_ref[...]
    z_ref[...] = _dot(x.astype(BF16), w_in_ref[...])
    wm = _masked_mix_weights(w_s_ref, stream_rows)

    def vn_fn(rows):
        parts = _sgu_norm(z_ref, sgu_g_ref, sgu_b_ref, rows)
        for h in range(N_SGU_HEADS):
            vn_ref[rows, h * SGU_HEAD_DIM:(h + 1) * SGU_HEAD_DIM] = parts[h]
        return parts

    def prev8_fn(s):
        return st_ref[s]

    _mixers(z_ref, wm, bias_ref, w_conv_ref, vn_fn, prev8_fn, mix_ref, 0, n_rows, stream_rows)
    for s in range(n_rows // stream_rows):
        r0 = (s + 1) * stream_rows - SUBLANES
        conv_ref[s] = (z_ref[r0:r0 + SUBLANES, _GC0:_GC0 + CONV_WIDTH]
                       * z_ref[r0:r0 + SUBLANES, _XI0:_XI0 + CONV_WIDTH])
    x1 = _layernorm(ALPHA * x + _dot(mix_ref[...], w_out_ref[...]), ln1g_ref[...], ln1b_ref[...])
    x1_ref[...] = x1
    q_ref[...] = _dot(x1.astype(BF16), w_q_ref[...]).astype(BF16)


def _sample_attn_kernel(q_ref, k_ref, v_ref, o_ref):
    _attn_values(_attn_scores(q_ref, k_ref, slice(None), _head_2d), v_ref, o_ref, slice(None), _head_2d)


def _sample_c_kernel(x1_ref, o_ref, w_o_ref, ln2g_ref, ln2b_ref, w_gate_ref, w_up_ref, w_down_ref,
                     ln3g_ref, ln3b_ref, y_ref, xb_ref, h_ref, res_ref):
    x2 = _layernorm(ALPHA * x1_ref[...] + _dot(o_ref[...], w_o_ref[...]), ln2g_ref[...], ln2b_ref[...])
    res_ref[...] = x2
    xb_ref[...] = x2.astype(BF16)
    _ffn(res_ref, xb_ref, w_gate_ref, w_up_ref, w_down_ref, ln3g_ref, ln3b_ref, h_ref, y_ref,
         slice(0, y_ref.shape[0]))


def _sample_layer(x, state8, mem_k, mem_v, p, bias):
    n_streams, stream_rows, _ = x.shape
    n_rows = n_streams * stream_rows
    x2d = x.reshape(n_rows, D_MODEL)
    a_in = (x2d, state8, p['w_in'], p['sgu_g'], p['sgu_b'], p['w_s'], bias, p['w_conv'], p['w_out'],
            p['ln1g'], p['ln1b'], p['w_q'])
    x1, q, vn, conv8 = pl.pallas_call(
        functools.partial(_sample_a_kernel, n_rows=n_rows, stream_rows=stream_rows),
        grid=(1,),
        in_specs=[_const_spec(a.shape) for a in a_in],
        out_specs=[_whole_spec((n_rows, D_MODEL)), _whole_spec((n_rows, D_MODEL)),
                   _whole_spec((n_rows, SGU_WIDTH)), _whole_spec((n_streams, SUBLANES, CONV_WIDTH))],
        out_shape=[jax.ShapeDtypeStruct((n_rows, D_MODEL), F32),
                   jax.ShapeDtypeStruct((n_rows, D_MODEL), BF16),
                   jax.ShapeDtypeStruct((n_rows, SGU_WIDTH), F32),
                   jax.ShapeDtypeStruct((n_streams, SUBLANES, CONV_WIDTH), F32)],
        scratch_shapes=[pltpu.VMEM((n_rows, IN_PROJ), F32), pltpu.VMEM((n_rows, D_MODEL), BF16)],
        compiler_params=pltpu.CompilerParams(dimension_semantics=("arbitrary",),
                                             vmem_limit_bytes=VMEM_LIMIT_BYTES),
        name="sample_mixers",
    )(*a_in)

    qspec = pl.BlockSpec((stream_rows, D_MODEL), lambda s: (s, 0))
    kvspec = pl.BlockSpec((None, N_MEM, D_MODEL), lambda s: (s, 0, 0))
    o = pl.pallas_call(
        _sample_attn_kernel, grid=(n_streams,), in_specs=[qspec, kvspec, kvspec], out_specs=qspec,
        out_shape=jax.ShapeDtypeStruct((n_rows, D_MODEL), BF16),
        compiler_params=pltpu.CompilerParams(dimension_semantics=("arbitrary",)),
        name="sample_attention",
    )(q, mem_k.reshape(n_streams, N_MEM, D_MODEL), mem_v.reshape(n_streams, N_MEM, D_MODEL))

    c_in = (x1, o, p['w_o'], p['ln2g'], p['ln2b'], p['w_gate'], p['w_up'], p['w_down'],
            p['ln3g'], p['ln3b'])
    y = pl.pallas_call(
        _sample_c_kernel, grid=(1,),
        in_specs=[_const_spec(a.shape) for a in c_in],
        out_specs=_whole_spec((n_rows, D_MODEL)),
        out_shape=jax.ShapeDtypeStruct((n_rows, D_MODEL), F32),
        scratch_shapes=[pltpu.VMEM((n_rows, D_MODEL), BF16), pltpu.VMEM((n_rows, D_FF), BF16),
                        pltpu.VMEM((n_rows, D_MODEL), F32)],
        compiler_params=pltpu.CompilerParams(dimension_semantics=("arbitrary",),
                                             vmem_limit_bytes=VMEM_LIMIT_BYTES),
        name="sample_ffn",
    )(*c_in)
    return y.reshape(x.shape), conv8, vn


def _row_bias(b_s, t):
    b = jnp.tile(b_s[:, :t], (1, SGU_CHUNK // t))
    return jnp.repeat(b.T, SGU_HEAD_DIM, axis=1)


def kernel(x_prompt, x_sample, cache_mem_k, cache_mem_v, state_conv, mem_prompt, w_in, sgu_ln_g, sgu_ln_b, w_s, b_s, w_conv, w_out, ln_mix_g, ln_mix_b, w_q, w_mem_k, w_mem_v, w_mem_o, ln_mem_g, ln_mem_b, w_gate, w_up, w_down, ln_ffn_g, ln_ffn_b):
    depth = w_in.shape[0]
    assert depth == 1
    bsz, seq, _ = x_prompt.shape
    n_streams, dec_seq, _ = x_sample.shape
    assert seq % TILE_ROWS == 0 and SGU_CHUNK % dec_seq == 0 and dec_seq % SUBLANES == 0
    assert (n_streams * dec_seq) % SGU_CHUNK == 0
    l = 0
    row = lambda a: a.reshape(1, -1)
    p = dict(w_in=w_in[l].astype(BF16), sgu_g=sgu_ln_g[l], sgu_b=sgu_ln_b[l], w_s=w_s[l],
             w_conv=w_conv[l], w_out=w_out[l].astype(BF16), ln1g=row(ln_mix_g[l]), ln1b=row(ln_mix_b[l]),
             w_q=w_q[l].astype(BF16), w_o=w_mem_o[l].astype(BF16), ln2g=row(ln_mem_g[l]),
             ln2b=row(ln_mem_b[l]), w_gate=w_gate[l].astype(BF16), w_up=w_up[l].astype(BF16),
             w_down=w_down[l].astype(BF16), ln3g=row(ln_ffn_g[l]), ln3b=row(ln_ffn_b[l]))

    mk, mv, mk_flat, mv_flat = _mem_kv(mem_prompt, w_mem_k[l].astype(BF16), w_mem_v[l].astype(BF16))
    y_prompt, conv8_p = _prompt_layer(x_prompt, mk_flat, mv_flat,
                                      dict(p, bias=_row_bias(b_s[l], SGU_CHUNK)))

    state8 = jnp.pad(state_conv[l], ((0, 0), (SUBLANES - (CONV_K - 1), 0), (0, 0)))
    y_sample, conv8_s, vn = _sample_layer(x_sample, state8, cache_mem_k[l], cache_mem_v[l],
                                          p, _row_bias(b_s[l], dec_seq))

    tail = slice(SUBLANES - (CONV_K - 1), SUBLANES)
    return (y_prompt, y_sample, mk[None], mv[None],
            conv8_p[None, :, tail], conv8_s[None, :, tail],
            vn.reshape(depth, n_streams, dec_seq, N_SGU_HEADS, SGU_HEAD_DIM))
```

```python
import functools

import jax
import jax.numpy as jnp
from jax import lax
from jax.experimental import pallas as pl
from jax.experimental.pallas import tpu as pltpu

D_MODEL = 1024
CHUNK = 64
SGU_CHUNK = 128
N_SGU_HEADS = 4
SGU_WIDTH = D_MODEL // 2
SGU_HEAD_DIM = SGU_WIDTH // N_SGU_HEADS
CONV_WIDTH = D_MODEL // 2
CONV_K = 3
IN_PROJ = 2 * SGU_WIDTH + 3 * CONV_WIDTH
N_MEM = 256
N_MEM_HEADS = 4
MEM_HEAD_DIM = D_MODEL // N_MEM_HEADS
D_FF = 2816
ALPHA = 2.0 ** 0.25
LN_EPS = 1e-5
ATTN_SCALE = MEM_HEAD_DIM ** -0.5

SUBLANES = 8
MXU_WIDTH = 256
TILE_ROWS = 512
BLOCK_ROWS = 256
LN_ROWS = 16
SOFTMAX_ROWS = 64
VMEM_LIMIT_BYTES = 60 * 1024 * 1024

F32 = jnp.float32
BF16 = jnp.bfloat16

_U0, _V0, _GB0, _GC0, _XI0 = 0, SGU_WIDTH, 2 * SGU_WIDTH, 2 * SGU_WIDTH + CONV_WIDTH, 2 * SGU_WIDTH + 2 * CONV_WIDTH


def _dot(a, b):
    return jnp.dot(a, b, preferred_element_type=F32)


def _layernorm(x, g, b):
    mu = jnp.mean(x, axis=-1, keepdims=True)
    xc = x - mu
    var = jnp.mean(xc * xc, axis=-1, keepdims=True)
    return xc * lax.rsqrt(var + LN_EPS) * g + b


def _layernorm_rows(ref, rows, g_ref, b_ref, bf16_ref=None):
    for r in range(rows.start, rows.stop, LN_ROWS):
        piece = slice(r, r + LN_ROWS)
        y = _layernorm(ref[piece, :], g_ref[...], b_ref[...])
        ref[piece, :] = y
        if bf16_ref is not None:
            bf16_ref[piece, :] = y.astype(BF16)


def _shift_rows(cur, prev8, k):
    rolled = pltpu.roll(cur, k, axis=0)
    head = pltpu.roll(prev8, k, axis=0)
    row = lax.broadcasted_iota(jnp.int32, (SUBLANES, cur.shape[1]), 0)
    first = jnp.where(row < k, head, rolled[:SUBLANES])
    return jnp.concatenate([first, rolled[SUBLANES:]], axis=0)


def _sgu_norm(z_ref, sgu_g_ref, sgu_b_ref, rows):
    parts = []
    for h in range(N_SGU_HEADS):
        c0 = _V0 + h * SGU_HEAD_DIM
        parts.append(_layernorm(z_ref[rows, c0:c0 + SGU_HEAD_DIM],
                                sgu_g_ref[h:h + 1, :], sgu_b_ref[h:h + 1, :]))
    return parts


def _head_2d(ref, h):
    return ref[:, h * MEM_HEAD_DIM:(h + 1) * MEM_HEAD_DIM].astype(BF16)


def _attn_scores(q_ref, k_ref, rows, head):
    scores = []
    for h in range(N_MEM_HEADS):
        c = slice(h * MEM_HEAD_DIM, (h + 1) * MEM_HEAD_DIM)
        scores.append(lax.dot_general(q_ref[rows, c], head(k_ref, h),
                                      (((1,), (1,)), ((), ())), preferred_element_type=F32) * ATTN_SCALE)
    return scores


def _attn_values(scores, v_ref, o_ref, rows, head):
    for h, sc in enumerate(scores):
        c = slice(h * MEM_HEAD_DIM, (h + 1) * MEM_HEAD_DIM)
        pieces = []
        for r in range(0, sc.shape[0], SOFTMAX_ROWS):
            s = sc[r:r + SOFTMAX_ROWS]
            e = jnp.exp(s - jnp.max(s, axis=-1, keepdims=True))
            pieces.append((e / jnp.sum(e, axis=-1, keepdims=True)).astype(BF16))
        p = jnp.concatenate(pieces, axis=0) if len(pieces) > 1 else pieces[0]
        o_ref[rows, c] = _dot(p, head(v_ref, h)).astype(BF16)


def _ffn(x2_ref, xb_ref, w_gate_ref, w_up_ref, w_down_ref, g_ref, b_ref, h_ref, y_ref, rows):
    for j in range(D_FF // MXU_WIDTH):
        c = slice(j * MXU_WIDTH, (j + 1) * MXU_WIDTH)
        gate = _dot(xb_ref[rows, :], w_gate_ref[:, c])
        up = _dot(xb_ref[rows, :], w_up_ref[:, c])
        h_ref[rows, c] = (gate * jax.nn.sigmoid(gate) * up).astype(BF16)
    k1 = D_FF - 2 * MXU_WIDTH
    f = _dot(h_ref[rows, :k1], w_down_ref[:k1, :]) + _dot(h_ref[rows, k1:], w_down_ref[k1:, :])
    y_ref[rows, :] = ALPHA * x2_ref[rows, :] + f
    _layernorm_rows(y_ref, rows, g_ref, b_ref)


def _masked_mix_weights(w_s_ref, t):
    row = lax.broadcasted_iota(jnp.int32, (SGU_CHUNK, SGU_CHUNK), 0)
    col = lax.broadcasted_iota(jnp.int32, (SGU_CHUNK, SGU_CHUNK), 1)
    chunk_of = lambda i: lax.shift_right_logical(i, CHUNK.bit_length() - 1)
    out = []
    for h in range(N_SGU_HEADS):
        w = w_s_ref[h]
        if t == SGU_CHUNK:
            keep = chunk_of(col) <= chunk_of(row)
        else:
            w = jnp.where(col < t, jnp.concatenate([w[:t]] * (SGU_CHUNK // t), axis=0), 0.0)
            acc = w
            for r in range(1, SGU_CHUNK // t):
                acc = acc + pltpu.roll(w, r * t, axis=1)
            w = acc
            block_of = lambda i: lax.shift_right_logical(i, t.bit_length() - 1)
            within = lambda i: i & (t - 1)
            keep = (block_of(col) == block_of(row)) & (chunk_of(within(col)) <= chunk_of(within(row)))
        out.append(jnp.where(keep, w, 0.0).astype(BF16))
    return out


def _mixers(z_ref, wm, bias_ref, w_conv_ref, vn_fn, prev8_fn, mix_ref, row0, n_rows, stream_rows):
    cx_prev = None
    for c in range(n_rows // SGU_CHUNK):
        rows = slice(row0 + c * SGU_CHUNK, row0 + (c + 1) * SGU_CHUNK)
        vn = vn_fn(rows)
        for h in range(N_SGU_HEADS):
            cols = slice(h * SGU_HEAD_DIM, (h + 1) * SGU_HEAD_DIM)
            mixed = _dot(wm[h], vn[h].astype(BF16)) + bias_ref[:, cols]
            u = z_ref[rows, _U0 + h * SGU_HEAD_DIM:_U0 + (h + 1) * SGU_HEAD_DIM]
            mix_ref[rows, cols] = (u * mixed).astype(BF16)
        cx = z_ref[rows, _GC0:_GC0 + CONV_WIDTH] * z_ref[rows, _XI0:_XI0 + CONV_WIDTH]
        if stream_rows >= SGU_CHUNK:
            prev8 = (prev8_fn((c * SGU_CHUNK) // stream_rows) if (c * SGU_CHUNK) % stream_rows == 0
                     else cx_prev[-SUBLANES:])
            s1 = _shift_rows(cx, prev8, 1)
            s2 = _shift_rows(cx, prev8, 2)
        else:
            s1p, s2p = [], []
            for s in range(SGU_CHUNK // stream_rows):
                blk = cx[s * stream_rows:(s + 1) * stream_rows]
                prev8 = prev8_fn(c * (SGU_CHUNK // stream_rows) + s)
                s1p.append(_shift_rows(blk, prev8, 1))
                s2p.append(_shift_rows(blk, prev8, 2))
            s1 = jnp.concatenate(s1p, axis=0)
            s2 = jnp.concatenate(s2p, axis=0)
        conv = s2 * w_conv_ref[0:1, :] + s1 * w_conv_ref[1:2, :] + cx * w_conv_ref[2:3, :]
        mix_ref[rows, SGU_WIDTH:] = (z_ref[rows, _GB0:_GB0 + CONV_WIDTH] * conv).astype(BF16)
        cx_prev = cx
    return cx_prev[-SUBLANES:]


def _prompt_kernel(x_ref, mk_ref, mv_ref, w_in_ref, sgu_g_ref, sgu_b_ref, w_s_ref, bias_ref,
                   w_conv_ref, w_out_ref, ln1g_ref, ln1b_ref, w_q_ref, w_o_ref, ln2g_ref, ln2b_ref,
                   w_gate_ref, w_up_ref, w_down_ref, ln3g_ref, ln3b_ref,
                   y_ref, conv_ref,
                   z_ref, act_a, act_b, h_ref, res_ref, carry_ref):
    @pl.when(pl.program_id(1) == 0)
    def _():
        carry_ref[...] = jnp.zeros_like(carry_ref)

    wm = _masked_mix_weights(w_s_ref, SGU_CHUNK)
    blocks = [slice(i * BLOCK_ROWS, (i + 1) * BLOCK_ROWS) for i in range(TILE_ROWS // BLOCK_ROWS)]
    prev8 = carry_ref[...]
    for rows in blocks:
        xb = x_ref[rows, :].astype(BF16)
        for cols in (slice(_V0, _GB0), slice(_U0, _V0), slice(_GB0, IN_PROJ)):
            z_ref[rows, cols] = _dot(xb, w_in_ref[:, cols])
        prev8 = _mixers(z_ref, wm, bias_ref, w_conv_ref,
                        lambda r: _sgu_norm(z_ref, sgu_g_ref, sgu_b_ref, r),
                        lambda _, prev8=prev8: prev8, act_a, rows.start, BLOCK_ROWS, BLOCK_ROWS)
    carry_ref[...] = prev8
    conv_ref[...] = prev8
    for rows in blocks:
        res_ref[rows, :] = ALPHA * x_ref[rows, :] + _dot(act_a[rows, :], w_out_ref[...])
        _layernorm_rows(res_ref, rows, ln1g_ref, ln1b_ref, act_b)
    scores = []
    for rows in blocks:
        act_a[rows, :] = _dot(act_b[rows, :], w_q_ref[...]).astype(BF16)
        scores.append(_attn_scores(act_a, mk_ref, rows, _head_2d))
    for rows, sc in zip(blocks, scores):
        _attn_values(sc, mv_ref, act_b, rows, _head_2d)
        res_ref[rows, :] = ALPHA * res_ref[rows, :] + _dot(act_b[rows, :], w_o_ref[...])
        _layernorm_rows(res_ref, rows, ln2g_ref, ln2b_ref, act_a)
    for rows in blocks:
        _ffn(res_ref, act_a, w_gate_ref, w_up_ref, w_down_ref, ln3g_ref, ln3b_ref, h_ref, y_ref, rows)


def _const_spec(shape):
    return pl.BlockSpec(shape, lambda *_: (0,) * len(shape), pipeline_mode=pl.Buffered(1))


def _whole_spec(shape):
    return pl.BlockSpec(shape, lambda *_: (0,) * len(shape))


def _prompt_layer(x, mk, mv, p):
    bsz, seq, _ = x.shape
    n_tiles = seq // TILE_ROWS
    weights = (p['w_in'], p['sgu_g'], p['sgu_b'], p['w_s'], p['bias'], p['w_conv'], p['w_out'],
               p['ln1g'], p['ln1b'], p['w_q'], p['w_o'], p['ln2g'], p['ln2b'],
               p['w_gate'], p['w_up'], p['w_down'], p['ln3g'], p['ln3b'])
    kv_spec = pl.BlockSpec((None, N_MEM, D_MODEL), lambda b, s: (b, 0, 0))
    return pl.pallas_call(
        _prompt_kernel,
        grid=(bsz, n_tiles),
        in_specs=[pl.BlockSpec((None, TILE_ROWS, D_MODEL), lambda b, s: (b, s, 0)), kv_spec, kv_spec]
        + [_const_spec(w.shape) for w in weights],
        out_specs=[pl.BlockSpec((None, TILE_ROWS, D_MODEL), lambda b, s: (b, s, 0)),
                   pl.BlockSpec((None, SUBLANES, CONV_WIDTH), lambda b, s: (b, 0, 0))],
        out_shape=[jax.ShapeDtypeStruct((bsz, seq, D_MODEL), F32),
                   jax.ShapeDtypeStruct((bsz, SUBLANES, CONV_WIDTH), F32)],
        scratch_shapes=[pltpu.VMEM((TILE_ROWS, IN_PROJ), F32),
                        pltpu.VMEM((TILE_ROWS, D_MODEL), BF16),
                        pltpu.VMEM((TILE_ROWS, D_MODEL), BF16),
                        pltpu.VMEM((TILE_ROWS, D_FF), BF16),
                        pltpu.VMEM((TILE_ROWS, D_MODEL), F32),
                        pltpu.VMEM((SUBLANES, CONV_WIDTH), F32)],
        compiler_params=pltpu.CompilerParams(dimension_semantics=("arbitrary", "arbitrary"),
                                             vmem_limit_bytes=VMEM_LIMIT_BYTES),
        name="prompt_layer",
    )(x, mk, mv, *weights)


def _mem_kv_kernel(m_ref, wk_ref, wv_ref, k_ref, v_ref, kb_ref, vb_ref):
    m = m_ref[...].astype(BF16)
    for w_ref, o_ref, ob_ref in ((wk_ref, k_ref, kb_ref), (wv_ref, v_ref, vb_ref)):
        kv = _dot(m, w_ref[...])
        ob_ref[...] = kv.astype(BF16)
        for h in range(N_MEM_HEADS):
            o_ref[:, h, :] = kv[:, h * MEM_HEAD_DIM:(h + 1) * MEM_HEAD_DIM]


def _mem_kv(mem, wk, wv):
    bsz = mem.shape[0]
    spec = pl.BlockSpec((None, N_MEM, D_MODEL), lambda b: (b, 0, 0))
    hspec = pl.BlockSpec((None, N_MEM, N_MEM_HEADS, MEM_HEAD_DIM), lambda b: (b, 0, 0, 0))
    wspec = pl.BlockSpec((D_MODEL, D_MODEL), lambda b: (0, 0))
    return pl.pallas_call(
        _mem_kv_kernel, grid=(bsz,), in_specs=[spec, wspec, wspec], out_specs=[hspec, hspec, spec, spec],
        out_shape=[jax.ShapeDtypeStruct((bsz, N_MEM, N_MEM_HEADS, MEM_HEAD_DIM), F32)] * 2
        + [jax.ShapeDtypeStruct((bsz, N_MEM, D_MODEL), BF16)] * 2,
        compiler_params=pltpu.CompilerParams(dimension_semantics=("arbitrary",)),
        name="mem_kv",
    )(mem, wk, wv)


def _sample_a_kernel(x_ref, st_ref, w_in_ref, sgu_g_ref, sgu_b_ref, w_s_ref, bias_ref, w_conv_ref,
                     w_out_ref, ln1g_ref, ln1b_ref, w_q_ref,
                     x1_ref, q_ref, vn_ref, conv_ref, z_ref, mix_ref, *, n_rows, stream_rows):
    x = x_ref[...]
    z_ref[...] = _dot(x.astype(BF16), w_in_ref[...])
    wm = _masked_mix_weights(w_s_ref, stream_rows)

    def vn_fn(rows):
        parts = _sgu_norm(z_ref, sgu_g_ref, sgu_b_ref, rows)
        for h in range(N_SGU_HEADS):
            vn_ref[rows, h * SGU_HEAD_DIM:(h + 1) * SGU_HEAD_DIM] = parts[h]
        return parts

    def prev8_fn(s):
        return st_ref[s]

    _mixers(z_ref, wm, bias_ref, w_conv_ref, vn_fn, prev8_fn, mix_ref, 0, n_rows, stream_rows)
    for s in range(n_rows // stream_rows):
        r0 = (s + 1) * stream_rows - SUBLANES
        conv_ref[s] = (z_ref[r0:r0 + SUBLANES, _GC0:_GC0 + CONV_WIDTH]
                       * z_ref[r0:r0 + SUBLANES, _XI0:_XI0 + CONV_WIDTH])
    x1 = _layernorm(ALPHA * x + _dot(mix_ref[...], w_out_ref[...]), ln1g_ref[...], ln1b_ref[...])
    x1_ref[...] = x1
    q_ref[...] = _dot(x1.astype(BF16), w_q_ref[...]).astype(BF16)


def _sample_attn_kernel(q_ref, k_ref, v_ref, o_ref):
    _attn_values(_attn_scores(q_ref, k_ref, slice(None), _head_2d), v_ref, o_ref, slice(None), _head_2d)


def _sample_c_kernel(x1_ref, o_ref, w_o_ref, ln2g_ref, ln2b_ref, w_gate_ref, w_up_ref, w_down_ref,
                     ln3g_ref, ln3b_ref, y_ref, xb_ref, h_ref, res_ref):
    x2 = _layernorm(ALPHA * x1_ref[...] + _dot(o_ref[...], w_o_ref[...]), ln2g_ref[...], ln2b_ref[...])
    res_ref[...] = x2
    xb_ref[...] = x2.astype(BF16)
    _ffn(res_ref, xb_ref, w_gate_ref, w_up_ref, w_down_ref, ln3g_ref, ln3b_ref, h_ref, y_ref,
         slice(0, y_ref.shape[0]))


def _sample_layer(x, state8, mem_k, mem_v, p, bias):
    n_streams, stream_rows, _ = x.shape
    n_rows = n_streams * stream_rows
    x2d = x.reshape(n_rows, D_MODEL)
    a_in = (x2d, state8, p['w_in'], p['sgu_g'], p['sgu_b'], p['w_s'], bias, p['w_conv'], p['w_out'],
            p['ln1g'], p['ln1b'], p['w_q'])
    x1, q, vn, conv8 = pl.pallas_call(
        functools.partial(_sample_a_kernel, n_rows=n_rows, stream_rows=stream_rows),
        grid=(1,),
        in_specs=[_const_spec(a.shape) for a in a_in],
        out_specs=[_whole_spec((n_rows, D_MODEL)), _whole_spec((n_rows, D_MODEL)),
                   _whole_spec((n_rows, SGU_WIDTH)), _whole_spec((n_streams, SUBLANES, CONV_WIDTH))],
        out_shape=[jax.ShapeDtypeStruct((n_rows, D_MODEL), F32),
                   jax.ShapeDtypeStruct((n_rows, D_MODEL), BF16),
                   jax.ShapeDtypeStruct((n_rows, SGU_WIDTH), F32),
                   jax.ShapeDtypeStruct((n_streams, SUBLANES, CONV_WIDTH), F32)],
        scratch_shapes=[pltpu.VMEM((n_rows, IN_PROJ), F32), pltpu.VMEM((n_rows, D_MODEL), BF16)],
        compiler_params=pltpu.CompilerParams(dimension_semantics=("arbitrary",),
                                             vmem_limit_bytes=VMEM_LIMIT_BYTES),
        name="sample_mixers",
    )(*a_in)

    qspec = pl.BlockSpec((stream_rows, D_MODEL), lambda s: (s, 0))
    kvspec = pl.BlockSpec((None, N_MEM, D_MODEL), lambda s: (s, 0, 0))
    o = pl.pallas_call(
        _sample_attn_kernel, grid=(n_streams,), in_specs=[qspec, kvspec, kvspec], out_specs=qspec,
        out_shape=jax.ShapeDtypeStruct((n_rows, D_MODEL), BF16),
        compiler_params=pltpu.CompilerParams(dimension_semantics=("arbitrary",)),
        name="sample_attention",
    )(q, mem_k.reshape(n_streams, N_MEM, D_MODEL).astype(BF16),
      mem_v.reshape(n_streams, N_MEM, D_MODEL).astype(BF16))

    c_in = (x1, o, p['w_o'], p['ln2g'], p['ln2b'], p['w_gate'], p['w_up'], p['w_down'],
            p['ln3g'], p['ln3b'])
    y = pl.pallas_call(
        _sample_c_kernel, grid=(1,),
        in_specs=[_const_spec(a.shape) for a in c_in],
        out_specs=_whole_spec((n_rows, D_MODEL)),
        out_shape=jax.ShapeDtypeStruct((n_rows, D_MODEL), F32),
        scratch_shapes=[pltpu.VMEM((n_rows, D_MODEL), BF16), pltpu.VMEM((n_rows, D_FF), BF16),
                        pltpu.VMEM((n_rows, D_MODEL), F32)],
        compiler_params=pltpu.CompilerParams(dimension_semantics=("arbitrary",),
                                             vmem_limit_bytes=VMEM_LIMIT_BYTES),
        name="sample_ffn",
    )(*c_in)
    return y.reshape(x.shape), conv8, vn


def _row_bias(b_s, t):
    b = jnp.tile(b_s[:, :t], (1, SGU_CHUNK // t))
    return jnp.repeat(b.T, SGU_HEAD_DIM, axis=1)


def kernel(x_prompt, x_sample, cache_mem_k, cache_mem_v, state_conv, mem_prompt, w_in, sgu_ln_g, sgu_ln_b, w_s, b_s, w_conv, w_out, ln_mix_g, ln_mix_b, w_q, w_mem_k, w_mem_v, w_mem_o, ln_mem_g, ln_mem_b, w_gate, w_up, w_down, ln_ffn_g, ln_ffn_b):
    depth = w_in.shape[0]
    assert depth == 1
    bsz, seq, _ = x_prompt.shape
    n_streams, dec_seq, _ = x_sample.shape
    assert seq % TILE_ROWS == 0 and SGU_CHUNK % dec_seq == 0 and dec_seq % SUBLANES == 0
    assert (n_streams * dec_seq) % SGU_CHUNK == 0
    l = 0
    row = lambda a: a.reshape(1, -1)
    p = dict(w_in=w_in[l].astype(BF16), sgu_g=sgu_ln_g[l], sgu_b=sgu_ln_b[l], w_s=w_s[l],
             w_conv=w_conv[l], w_out=w_out[l].astype(BF16), ln1g=row(ln_mix_g[l]), ln1b=row(ln_mix_b[l]),
             w_q=w_q[l].astype(BF16), w_o=w_mem_o[l].astype(BF16), ln2g=row(ln_mem_g[l]),
             ln2b=row(ln_mem_b[l]), w_gate=w_gate[l].astype(BF16), w_up=w_up[l].astype(BF16),
             w_down=w_down[l].astype(BF16), ln3g=row(ln_ffn_g[l]), ln3b=row(ln_ffn_b[l]))

    mk, mv, mk_flat, mv_flat = _mem_kv(mem_prompt, w_mem_k[l].astype(BF16), w_mem_v[l].astype(BF16))
    y_prompt, conv8_p = _prompt_layer(x_prompt, mk_flat, mv_flat,
                                      dict(p, bias=_row_bias(b_s[l], SGU_CHUNK)))

    state8 = jnp.pad(state_conv[l], ((0, 0), (SUBLANES - (CONV_K - 1), 0), (0, 0)))
    y_sample, conv8_s, vn = _sample_layer(x_sample, state8, cache_mem_k[l], cache_mem_v[l],
                                          p, _row_bias(b_s[l], dec_seq))

    tail = slice(SUBLANES - (CONV_K - 1), SUBLANES)
    return (y_prompt, y_sample, mk[None], mv[None],
            conv8_p[None, :, tail], conv8_s[None, :, tail],
            vn.reshape(depth, n_streams, dec_seq, N_SGU_HEADS, SGU_HEAD_DIM))
```
